```python
import jax
import jax.numpy as jnp
from jax import lax
import numpy as np

D_MODEL = 2048
BATCH = 4
SEQ = 4096
DEPTH = 2

D_MIX = D_MODEL
ATTN_HEAD_DIM = 64
ATTN_Q_HEADS = (D_MIX // 2) // ATTN_HEAD_DIM
ATTN_KV_HEADS = ATTN_Q_HEADS // 8
ATTN_Q_WIDTH = ATTN_Q_HEADS * ATTN_HEAD_DIM
ATTN_KV_WIDTH = ATTN_KV_HEADS * ATTN_HEAD_DIM
WINDOW = 128
ATTN_BLOCK = WINDOW
ROPE_THETA = 10000.0
HGRN_HEAD_DIM = 128
HGRN_HEADS = (D_MIX // 4) // HGRN_HEAD_DIM
HGRN_WIDTH = HGRN_HEADS * HGRN_HEAD_DIM
MLSTM_HEAD_DIM = 128
MLSTM_HEADS = (D_MIX // 4) // MLSTM_HEAD_DIM
MLSTM_WIDTH = MLSTM_HEADS * MLSTM_HEAD_DIM
MLSTM_CONV = 4
CHUNK = 64
IN_SIZES = (ATTN_Q_WIDTH, ATTN_KV_WIDTH, ATTN_KV_WIDTH,
            HGRN_WIDTH, HGRN_WIDTH, HGRN_WIDTH, HGRN_WIDTH,
            2 * MLSTM_WIDTH, MLSTM_WIDTH, MLSTM_WIDTH, MLSTM_HEADS, MLSTM_HEADS)
IN_DIM = sum(IN_SIZES)
MIX_OUT = ATTN_Q_WIDTH + HGRN_WIDTH + MLSTM_WIDTH
PEER_HEADS = 8
PEER_NKEYS = 128
PEER_EXPERTS = PEER_NKEYS * PEER_NKEYS
PEER_TOPK = 16
PEER_KEY_DIM = 256
PEER_TOKEN_BLOCK = 128
NORM_EPS = 1e-6
NEG_BIG = -1e30
TINY = 1e-30

kernel_name = "hymba_hgrn2_mlstm_swa_peer_trunk"


def rms_norm(x, g):
    xf = x.astype(jnp.float32)
    y = xf * lax.rsqrt(jnp.mean(xf * xf, axis=-1, keepdims=True) + NORM_EPS)
    return (y * g.astype(jnp.float32)).astype(x.dtype)


def split_cols(t, sizes):
    return jnp.split(t, np.cumsum(sizes)[:-1].tolist(), axis=-1)


def rotary(x, pos):
    half = x.shape[-1] // 2
    inv = ROPE_THETA ** (-jnp.arange(half, dtype=jnp.float32) / half)
    ang = pos.astype(jnp.float32)[:, None] * inv[None, :]
    cos = jnp.cos(ang)[None, :, None, :]
    sin = jnp.sin(ang)[None, :, None, :]
    xf = x.astype(jnp.float32)
    x1, x2 = xf[..., :half], xf[..., half:]
    return jnp.concatenate([x1 * cos - x2 * sin, x2 * cos + x1 * sin], axis=-1).astype(x.dtype)


def causal_depthwise_conv(x, w):
    k_width, s = w.shape[0], x.shape[1]
    xp = jnp.pad(x, ((0, 0), (k_width - 1, 0), (0, 0)))
    return sum(xp[:, j:j + s] * w[j] for j in range(k_width))


def to_chunks(t, c):
    b, s, h = t.shape[:3]
    t = t.reshape((b, s // c, c, h) + t.shape[3:])
    return jnp.moveaxis(t, (1, 3), (0, 2))


def from_chunks(t):
    t = jnp.moveaxis(t, (0, 2), (1, 3))
    b, n, c, h, d = t.shape
    return t.reshape(b, n * c, h, d)


def sliding_window_attention(q, k, v, sinks):
    b, s, hq, dh = q.shape
    hkv = k.shape[2]
    g = hq // hkv
    w = ATTN_BLOCK
    n = s // w
    qb = q.reshape(b, n, w, hkv, g, dh)

    def band(t):
        tb = t.reshape(b, n, w, hkv, dh)
        prev = jnp.pad(tb, ((0, 0), (1, 0), (0, 0), (0, 0), (0, 0)))[:, :-1]
        return jnp.concatenate([prev, tb], axis=2)

    kb, vb = band(k), band(v)
    logits = jnp.einsum('bnqhgd,bnkhd->bnhgqk', qb, kb).astype(jnp.float32) * (dh ** -0.5)
    qi = jnp.arange(w)[:, None]
    kj = jnp.arange(2 * w)[None, :]
    rel = qi + w - kj
    blk = jnp.arange(n)[:, None, None]
    valid = (rel >= 0) & (rel < WINDOW) & (blk * w - w + kj >= 0)
    logits = jnp.where(valid[None, :, None, None], logits, NEG_BIG)
    sink = jnp.broadcast_to(sinks.astype(jnp.float32).reshape(1, 1, hkv, g, 1, 1),
                            logits.shape[:-1] + (1,))
    probs = jax.nn.softmax(jnp.concatenate([logits, sink], axis=-1), axis=-1)[..., :-1]
    out = jnp.einsum('bnhgqk,bnkhd->bnqhgd', probs.astype(v.dtype), vb)
    return out.reshape(b, s, hq * dh)


def hgrn2_chunkwise(q, k, v, log_f):
    b, s, h, dk = q.shape
    dv = v.shape[-1]
    qc, kc, vc, ac = (to_chunks(t.astype(jnp.float32), CHUNK) for t in (q, k, v, log_f))
    causal = jnp.tril(jnp.ones((CHUNK, CHUNK), dtype=bool))[:, :, None]

    def step(state, inp):
        qi, ki, vi, ai = inp
        cum = jnp.cumsum(ai, axis=2)
        o_inter = jnp.einsum('bhtk,bhkv->bhtv', qi * jnp.exp(cum), state)
        rel = jnp.where(causal, cum[:, :, :, None, :] - cum[:, :, None, :, :], NEG_BIG)
        scores = jnp.einsum('bhtk,bhsk,bhtsk->bhts', qi, ki, jnp.exp(rel))
        o_intra = jnp.einsum('bhts,bhsv->bhtv', scores, vi)
        last = cum[:, :, -1]
        k_dec = ki * jnp.exp(last[:, :, None, :] - cum)
        state = jnp.exp(last)[..., None] * state + jnp.einsum('bhsk,bhsv->bhkv', k_dec, vi)
        return state, o_inter + o_intra

    init = jnp.zeros((b, h, dk, dv), jnp.float32)
    _, out = lax.scan(step, init, (qc, kc, vc, ac))
    return from_chunks(out)


def mlstm_chunkwise(q, k, v, i_pre, log_f):
    b, s, h, dk = q.shape
    dv = v.shape[-1]
    qc, kc, vc, ic, fc = (to_chunks(t.astype(jnp.float32), CHUNK) for t in (q, k, v, i_pre, log_f))
    causal = jnp.tril(jnp.ones((CHUNK, CHUNK), dtype=bool))

    def step(carry, inp):
        c_st, n_st, m = carry
        qi, ki, vi, ii, fi = inp
        cum = jnp.cumsum(fi, axis=-1)
        log_d = jnp.where(causal, cum[..., :, None] - cum[..., None, :] + ii[..., None, :], NEG_BIG)
        log_inter = cum + m[..., None]
        m_t = jnp.maximum(jnp.max(log_d, axis=-1), log_inter)
        w_intra = jnp.exp(log_d - m_t[..., None])
        w_inter = jnp.exp(log_inter - m_t)
        qk = jnp.einsum('bhtd,bhsd->bhts', qi, ki) * w_intra
        num = jnp.einsum('bhts,bhsv->bhtv', qk, vi) \
            + w_inter[..., None] * jnp.einsum('bhtd,bhdv->bhtv', qi, c_st)
        den = jnp.sum(qk, axis=-1) + w_inter * jnp.einsum('bhtd,bhd->bht', qi, n_st)
        h_out = num / jnp.maximum(jnp.abs(den), jnp.exp(-m_t))[..., None]
        last = cum[..., -1]
        log_w = last[..., None] - cum + ii
        m_new = jnp.maximum(last + m, jnp.max(log_w, axis=-1))
        w_s = jnp.exp(log_w - m_new[..., None])
        decay = jnp.exp(last + m - m_new)
        c_new = decay[..., None, None] * c_st + jnp.einsum('bhs,bhsd,bhsv->bhdv', w_s, ki, vi)
        n_new = decay[..., None] * n_st + jnp.einsum('bhs,bhsd->bhd', w_s, ki)
        return (c_new, n_new, m_new), h_out

    init = (jnp.zeros((b, h, dk, dv), jnp.float32), jnp.zeros((b, h, dk), jnp.float32),
            jnp.zeros((b, h), jnp.float32))
    _, out = lax.scan(step, init, (qc, kc, vc, ic, fc))
    return from_chunks(out)


def hybrid_mixer(h, w_in, w_out, sinks, lower_bound, hgrn_gain, conv_w, i_bias, f_bias):
    b, s, _ = h.shape
    pos = jnp.arange(s)
    (qa, ka, va, qh, fh, ih, gh, qkm, vm, om, im, fm) = split_cols(h @ w_in, IN_SIZES)

    qa = rotary(qa.reshape(b, s, ATTN_Q_HEADS, ATTN_HEAD_DIM), pos)
    ka = rotary(ka.reshape(b, s, ATTN_KV_HEADS, ATTN_HEAD_DIM), pos)
    va = va.reshape(b, s, ATTN_KV_HEADS, ATTN_HEAD_DIM)
    y_attn = sliding_window_attention(qa, ka, va, sinks)

    hs = (b, s, HGRN_HEADS, HGRN_HEAD_DIM)
    fz = fh.astype(jnp.float32)
    lb = lower_bound.astype(jnp.float32)
    f_gate = lb + (1.0 - lb) * jax.nn.sigmoid(fz)
    log_f = jnp.log(jnp.maximum(f_gate, TINY))
    k_h = (1.0 - lb) * jax.nn.sigmoid(-fz)
    o_h = hgrn2_chunkwise(jax.nn.silu(qh).reshape(hs), k_h.reshape(hs), ih.reshape(hs), log_f.reshape(hs))
    y_hgrn = rms_norm(o_h.astype(h.dtype), hgrn_gain.reshape(HGRN_HEADS, HGRN_HEAD_DIM))
    y_hgrn = y_hgrn.reshape(b, s, HGRN_WIDTH) * jax.nn.silu(gh)

    qk = jax.nn.silu(causal_depthwise_conv(qkm, conv_w))
    qm, km = jnp.split(qk, 2, axis=-1)
    ms = (b, s, MLSTM_HEADS, MLSTM_HEAD_DIM)
    o_m = mlstm_chunkwise(qm.reshape(ms), (km * MLSTM_HEAD_DIM ** -0.5).reshape(ms), vm.reshape(ms),
                          im + i_bias, jax.nn.log_sigmoid((fm + f_bias).astype(jnp.float32)))
    y_mlstm = jax.nn.sigmoid(om) * o_m.reshape(b, s, MLSTM_WIDTH).astype(h.dtype)

    y = jnp.concatenate([y_attn, y_hgrn, y_mlstm], axis=-1)
    return y @ w_out


def peer_ffn(x, w_query, sub_keys, expert_u, expert_v):
    b, s, d = x.shape
    xt = x.reshape(-1, d)
    t = xt.shape[0]
    q = (xt @ w_query).reshape(t, PEER_HEADS, 2, PEER_KEY_DIM // 2)
    s1 = jnp.einsum('thd,hnd->thn', q[:, :, 0], sub_keys[0]).astype(jnp.float32)
    s2 = jnp.einsum('thd,hnd->thn', q[:, :, 1], sub_keys[1]).astype(jnp.float32)
    v1, i1 = lax.top_k(s1, PEER_TOPK)
    v2, i2 = lax.top_k(s2, PEER_TOPK)
    cand = (v1[..., :, None] + v2[..., None, :]).reshape(t, PEER_HEADS, PEER_TOPK * PEER_TOPK)
    cidx = (i1[..., :, None] * PEER_NKEYS + i2[..., None, :]).reshape(t, PEER_HEADS, PEER_TOPK * PEER_TOPK)
    top_s, sel = lax.top_k(cand, PEER_TOPK)
    idx = jnp.take_along_axis(cidx, sel, axis=-1)
    gate = jax.nn.softmax(top_s, axis=-1)

    def block(args):
        xb, ib, gb = args
        pre = jnp.einsum('td,thkd->thk', xb, expert_u[ib])
        act = jax.nn.gelu(pre.astype(jnp.float32), approximate=False) * gb
        return jnp.einsum('thk,thkd->td', act.astype(xb.dtype), expert_v[ib])

    nb = t // PEER_TOKEN_BLOCK
    out = lax.map(block, (xt.reshape(nb, PEER_TOKEN_BLOCK, d),
                          idx.reshape(nb, PEER_TOKEN_BLOCK, PEER_HEADS, PEER_TOPK),
                          gate.reshape(nb, PEER_TOKEN_BLOCK, PEER_HEADS, PEER_TOPK)))
    return out.reshape(b, s, d)


def setup_inputs(seed: int = 0) -> dict:
    key = jax.random.key(seed)
    ks = jax.random.split(key, 17)

    def nrm(k, shape, scale):
        return jax.random.normal(k, shape, jnp.float32) * scale

    return {
        'x': nrm(ks[0], (BATCH, SEQ, D_MODEL), 1.0),
        'norm_mix': 1.0 + nrm(ks[1], (DEPTH, D_MODEL), 0.02),
        'w_in': nrm(ks[2], (DEPTH, D_MODEL, IN_DIM), D_MODEL ** -0.5),
        'w_out': nrm(ks[3], (DEPTH, MIX_OUT, D_MODEL), MIX_OUT ** -0.5),
        'attn_sinks': nrm(ks[4], (DEPTH, ATTN_Q_HEADS), 0.5),
        'hgrn_lb_logits': nrm(ks[5], (DEPTH, HGRN_WIDTH), 0.5),
        'hgrn_out_norm': 1.0 + nrm(ks[6], (DEPTH, HGRN_WIDTH), 0.02),
        'mlstm_conv': nrm(ks[7], (DEPTH, MLSTM_CONV, 2 * MLSTM_WIDTH), MLSTM_CONV ** -0.5),
        'mlstm_i_bias': nrm(ks[8], (DEPTH, MLSTM_HEADS), 0.1),
        'mlstm_f_bias': jnp.linspace(3.0, 6.0, MLSTM_HEADS, dtype=jnp.float32)[None, :]
                        + nrm(ks[9], (DEPTH, MLSTM_HEADS), 0.1),
        'norm_ffn': 1.0 + nrm(ks[10], (DEPTH, D_MODEL), 0.02),
        'peer_query': nrm(ks[11], (DEPTH, D_MODEL, PEER_HEADS * PEER_KEY_DIM), D_MODEL ** -0.5),
        'peer_sub_keys': nrm(ks[12], (DEPTH, 2, PEER_HEADS, PEER_NKEYS, PEER_KEY_DIM // 2),
                             (PEER_KEY_DIM // 2) ** -0.5),
        'peer_u': nrm(ks[13], (DEPTH, PEER_EXPERTS, D_MODEL), D_MODEL ** -0.5),
        'peer_v': nrm(ks[14], (DEPTH, PEER_EXPERTS, D_MODEL), PEER_TOPK ** -0.5),
        'norm_final': 1.0 + nrm(ks[15], (D_MODEL,), 0.02),
    }


def reference(x, norm_mix, w_in, w_out, attn_sinks, hgrn_lb_logits, hgrn_out_norm, mlstm_conv,
              mlstm_i_bias, mlstm_f_bias, norm_ffn, peer_query, peer_sub_keys, peer_u, peer_v,
              norm_final):
    p = jax.nn.softmax(hgrn_lb_logits.astype(jnp.float32), axis=0)
    lower_bounds = jnp.cumsum(p, axis=0) - p[0:1]
    for l in range(DEPTH):
        h = rms_norm(x, norm_mix[l])
        x = x + hybrid_mixer(h, w_in[l], w_out[l], attn_sinks[l], lower_bounds[l], hgrn_out_norm[l],
                             mlstm_conv[l], mlstm_i_bias[l], mlstm_f_bias[l])
        h = rms_norm(x, norm_ffn[l])
        x = x + peer_ffn(h, peer_query[l], peer_sub_keys[l], peer_u[l], peer_v[l])
    return rms_norm(x, norm_final)
```

```python
import functools

import jax
import jax.numpy as jnp
import numpy as np
from jax import lax
from jax.experimental import pallas as pl
from jax.experimental.pallas import tpu as pltpu

F32 = jnp.float32
BF16 = jnp.bfloat16
HIGHEST = lax.Precision.HIGHEST

D_MODEL = 2048
ATTN_HEAD_DIM = 64
ATTN_Q_HEADS = 16
ATTN_KV_HEADS = 2
ATTN_GROUP = ATTN_Q_HEADS // ATTN_KV_HEADS
WINDOW = 128
ROPE_THETA = 10000.0
REC_HEADS = 4
REC_DIM = 128
REC_WIDTH = REC_HEADS * REC_DIM
MLSTM_CONV = 4
CHUNK = 64
PEER_HEADS = 8
PEER_NKEYS = 128
PEER_TOPK = 16
PEER_HALF = 128
PEER_SLOTS = PEER_HEADS * PEER_TOPK
NORM_EPS = 1e-6
NEG_BIG = -1e30
TINY = 1e-30

LANES = 128
VMEM_LIMIT = 48 * 1024 * 1024

OFF_QA, OFF_KA, OFF_VA = 0, 1024, 1152
OFF_QH, OFF_FH, OFF_IH, OFF_GH = 1280, 1792, 2304, 2816
OFF_QKM, OFF_VM, OFF_OM, OFF_GATE = 3328, 4352, 4864, 5376
IN_DIM = 5384
IN_PAD = 5632


def _cparams(*sem):
    return pltpu.CompilerParams(dimension_semantics=sem, vmem_limit_bytes=VMEM_LIMIT)


def _sigmoid(x):
    return 1.0 / (1.0 + jnp.exp(-x))


def _silu(x):
    return x * _sigmoid(x)


def _norm_matmul_kernel(x_ref, g_ref, w_ref, o_ref, h_ref):
    @pl.when(pl.program_id(1) == 0)
    def _():
        x = x_ref[...]
        y = x * lax.rsqrt(jnp.mean(x * x, axis=-1, keepdims=True) + NORM_EPS) * g_ref[...]
        h_ref[...] = y.astype(BF16)

    o_ref[...] = jnp.dot(h_ref[...], w_ref[...], preferred_element_type=F32)


def _norm_matmul(x, g, w, *, tm, tn):
    t, d = x.shape
    n = w.shape[1]
    return pl.pallas_call(
        _norm_matmul_kernel,
        grid=(t // tm, n // tn),
        in_specs=[
            pl.BlockSpec((tm, d), lambda i, j: (i, 0)),
            pl.BlockSpec((1, d), lambda i, j: (0, 0)),
            pl.BlockSpec((d, tn), lambda i, j: (0, j)),
        ],
        out_specs=pl.BlockSpec((tm, tn), lambda i, j: (i, j)),
        out_shape=jax.ShapeDtypeStruct((t, n), F32),
        scratch_shapes=[pltpu.VMEM((tm, d), BF16)],
        compiler_params=_cparams("parallel", "arbitrary"),
        name="norm_matmul",
    )(x, g.reshape(1, d), w)


def _attn_kernel(sink_ref, q_ref, kc_ref, kp_ref, vc_ref, vp_ref, cc_ref, sc_ref, cp_ref, sp_ref, o_ref):
    n = pl.program_id(1)
    w = WINDOW
    lane = lax.broadcasted_iota(jnp.int32, (w, LANES), 1)
    first_half = (lane % ATTN_HEAD_DIM) < (ATTN_HEAD_DIM // 2)

    def rope(x, cos, sin_signed):
        rot = jnp.where(first_half, pltpu.roll(x, LANES - 32, 1), pltpu.roll(x, 32, 1))
        return x * cos + rot * sin_signed

    cos_c, sin_c = cc_ref[...], sc_ref[...]
    k_band = jnp.concatenate(
        [rope(kp_ref[...], cp_ref[...], sp_ref[...]), rope(kc_ref[...], cos_c, sin_c)], axis=0)
    v_band = jnp.concatenate([vp_ref[...], vc_ref[...]], axis=0)
    lane2 = lax.broadcasted_iota(jnp.int32, (2 * w, LANES), 1)
    low2 = lane2 < ATTN_HEAD_DIM

    def dup(x):
        sw = pltpu.roll(x, ATTN_HEAD_DIM, 1)
        return jnp.where(low2, x, sw).astype(BF16), jnp.where(low2, sw, x).astype(BF16)

    k_dup, v_dup = dup(k_band), dup(v_band)

    qi = lax.broadcasted_iota(jnp.int32, (w, 2 * w), 0)
    kj = lax.broadcasted_iota(jnp.int32, (w, 2 * w), 1)
    rel = qi + w - kj
    valid = (rel >= 0) & (rel < WINDOW) & ((kj >= w) | (n > 0))
    low = lane < ATTN_HEAD_DIM

    for c in range(ATTN_Q_HEADS // 2):
        qc = rope(q_ref[:, c * LANES:(c + 1) * LANES], cos_c, sin_c)
        kvh = (2 * c) // ATTN_GROUP
        outs = []
        for j in range(2):
            head = 2 * c + j
            qm = jnp.where(low if j == 0 else ~low, qc, 0.0).astype(BF16)
            logits = lax.dot_general(qm, k_dup[kvh], (((1,), (1,)), ((), ())), preferred_element_type=F32)
            logits = jnp.where(valid, logits * (ATTN_HEAD_DIM ** -0.5), NEG_BIG)
            sink = sink_ref[head]
            m = jnp.maximum(jnp.max(logits, axis=-1, keepdims=True), sink)
            p = jnp.exp(logits - m)
            denom = jnp.sum(p, axis=-1, keepdims=True) + jnp.exp(sink - m)
            o = jnp.dot(p.astype(BF16), v_dup[kvh], preferred_element_type=F32)
            outs.append(o / denom)
        o_ref[:, c * LANES:(c + 1) * LANES] = jnp.where(low, outs[0], outs[1]).astype(o_ref.dtype)


def _attention(proj, sinks, cos_t, sin_t, batch, seq):
    nblk = seq // WINDOW
    kcol, vcol = OFF_KA // LANES, OFF_VA // LANES

    def cur(col):
        return lambda b, n: (b * nblk + n, col)

    def prev(col):
        return lambda b, n: (b * nblk + jnp.maximum(n - 1, 0), col)

    blk = (WINDOW, LANES)
    return pl.pallas_call(
        _attn_kernel,
        grid=(batch, nblk),
        in_specs=[
            pl.BlockSpec(memory_space=pltpu.SMEM),
            pl.BlockSpec((WINDOW, ATTN_Q_HEADS * ATTN_HEAD_DIM), cur(0)),
            pl.BlockSpec(blk, cur(kcol)),
            pl.BlockSpec(blk, prev(kcol)),
            pl.BlockSpec(blk, cur(vcol)),
            pl.BlockSpec(blk, prev(vcol)),
            pl.BlockSpec(blk, lambda b, n: (n, 0)),
            pl.BlockSpec(blk, lambda b, n: (n, 0)),
            pl.BlockSpec(blk, lambda b, n: (jnp.maximum(n - 1, 0), 0)),
            pl.BlockSpec(blk, lambda b, n: (jnp.maximum(n - 1, 0), 0)),
        ],
        out_specs=pl.BlockSpec((WINDOW, ATTN_Q_HEADS * ATTN_HEAD_DIM), cur(0)),
        out_shape=jax.ShapeDtypeStruct((batch * seq, ATTN_Q_HEADS * ATTN_HEAD_DIM), BF16),
        compiler_params=_cparams("parallel", "arbitrary"),
        name="swa_attention",
    )(sinks, proj, proj, proj, proj, proj, cos_t, sin_t, cos_t, sin_t)


def _rope_tables(seq):
    half = ATTN_HEAD_DIM // 2
    inv = ROPE_THETA ** (-jnp.arange(half, dtype=F32) / half)
    ang = jnp.arange(seq, dtype=F32)[:, None] * inv[None, :]
    cos, sin = jnp.cos(ang), jnp.sin(ang)
    return jnp.tile(cos, (1, 4)), jnp.concatenate([-sin, sin, -sin, sin], axis=1)


def _hgrn_constants():
    c = CHUNK
    levels = int(np.log2(c))
    t = np.arange(c)[:, None]
    r = np.arange(c)[None, :]
    dq, dk, masks = [], [], [np.eye(c)]
    for lv in range(1, levels + 1):
        half = 1 << (lv - 1)
        upper = (t & half) != 0
        mid_q = t & ~(half - 1)
        dq.append((upper & (r >= mid_q) & (r <= t)).astype(np.float32))
        mid_k = (t & ~(half - 1)) + half
        dk.append((~upper & (r >= t + 1) & (r <= mid_k - 1)).astype(np.float32))
        same = (t >> lv) == (r >> lv)
        masks.append((same & upper & ((r & half) == 0)).astype(np.float32))
    tril = (r <= t).astype(np.float32)
    rem = (r > t).astype(np.float32)
    dall = np.concatenate(dq + dk + [tril, rem], axis=0)
    return jnp.asarray(dall), jnp.asarray(np.concatenate(masks, axis=0))


def _hgrn_kernel(q_ref, f_ref, i_ref, g_ref, lb_ref, gain_ref, dall_ref, mask_ref, o_ref, st_ref, *, sblk):
    c = CHUNK
    levels = int(np.log2(c))

    @pl.when(pl.program_id(2) == 0)
    def _():
        st_ref[...] = jnp.zeros_like(st_ref)

    lb = lb_ref[...]
    gain = gain_ref[...]
    dall = dall_ref[...]
    nt = (((1,), (1,)), ((), ()))

    def chunk(ci, carry):
        rows = pl.ds(pl.multiple_of(ci * c, c), c)
        fz = f_ref[rows, :]
        f_gate = lb + (1.0 - lb) * _sigmoid(fz)
        a = jnp.log(jnp.maximum(f_gate, TINY))
        k = (1.0 - lb) * _sigmoid(-fz)
        q = _silu(q_ref[rows, :])
        v = i_ref[rows, :]
        e = jnp.exp(jnp.dot(dall, a, precision=HIGHEST, preferred_element_type=F32))
        base = 2 * levels * c
        e_cum = e[base:base + c]
        e_rem = e[base + c:base + 2 * c]

        scores = mask_ref[0:c, :] * lax.dot_general(q.astype(BF16), k.astype(BF16), nt, preferred_element_type=F32)
        for lv in range(levels):
            qd = (q * e[lv * c:(lv + 1) * c]).astype(BF16)
            kd = (k * e[(levels + lv) * c:(levels + lv + 1) * c]).astype(BF16)
            scores = scores + mask_ref[(lv + 1) * c:(lv + 2) * c, :] * lax.dot_general(
                qd, kd, nt, preferred_element_type=F32)

        st = st_ref[...]
        o = jnp.dot(scores.astype(BF16), v.astype(BF16), preferred_element_type=F32)
        o = o + lax.dot_general((q * e_cum).astype(BF16), st.astype(BF16), nt, preferred_element_type=F32)
        k_dec = (k * e_rem).astype(BF16)
        st_ref[...] = st * e_cum[c - 1:c, :] + jnp.dot(v.T.astype(BF16), k_dec, preferred_element_type=F32)

        y = o * lax.rsqrt(jnp.mean(o * o, axis=-1, keepdims=True) + NORM_EPS) * gain
        o_ref[rows, :] = (y * _silu(g_ref[rows, :])).astype(o_ref.dtype)
        return carry

    lax.fori_loop(0, sblk // c, chunk, 0)


def _hgrn(proj, lower_bound, gain, batch, seq, sblk):
    ns = seq // sblk
    dall, masks = _hgrn_constants()

    def col(off):
        return lambda b, h, s: (b * ns + s, off // LANES + h)

    blk = (sblk, REC_DIM)
    vec = pl.BlockSpec((1, REC_DIM), lambda b, h, s: (0, h))
    return pl.pallas_call(
        functools.partial(_hgrn_kernel, sblk=sblk),
        grid=(batch, REC_HEADS, ns),
        in_specs=[
            pl.BlockSpec(blk, col(OFF_QH)),
            pl.BlockSpec(blk, col(OFF_FH)),
            pl.BlockSpec(blk, col(OFF_IH)),
            pl.BlockSpec(blk, col(OFF_GH)),
            vec,
            vec,
            pl.BlockSpec(dall.shape, lambda b, h, s: (0, 0)),
            pl.BlockSpec(masks.shape, lambda b, h, s: (0, 0)),
        ],
        out_specs=pl.BlockSpec(blk, lambda b, h, s: (b * ns + s, h)),
        out_shape=jax.ShapeDtypeStruct((batch * seq, REC_WIDTH), BF16),
        scratch_shapes=[pltpu.VMEM((REC_DIM, REC_DIM), F32)],
        compiler_params=_cparams("parallel", "parallel", "arbitrary"),
        name="hgrn2_scan",
    )(proj, proj, proj, proj, lower_bound.reshape(1, REC_WIDTH), gain.reshape(1, REC_WIDTH), dall, masks)


def _mlstm_kernel(bias_ref, qx_ref, kx_ref, v_ref, og_ref, gcol_ref, grow_ref, wq_ref, wk_ref, o_ref,
                  qbuf, kbuf, c_ref, n_ref, m_ref, *, sblk):
    c = CHUNK
    h = pl.program_id(1)
    pad = 8

    @pl.when(pl.program_id(2) == 0)
    def _():
        qbuf[0:pad, :] = jnp.zeros((pad, REC_DIM), F32)
        kbuf[0:pad, :] = jnp.zeros((pad, REC_DIM), F32)
        c_ref[...] = jnp.zeros_like(c_ref)
        n_ref[...] = jnp.zeros_like(n_ref)
        m_ref[...] = jnp.zeros_like(m_ref)

    def conv_silu(x_ref, buf, w_ref):
        buf[pad:pad + sblk, :] = x_ref[...]
        acc = jnp.zeros((sblk, REC_DIM), F32)
        for j in range(MLSTM_CONV):
            acc = acc + buf[pad - (MLSTM_CONV - 1) + j:pad - (MLSTM_CONV - 1) + j + sblk, :] * w_ref[j:j + 1, :]
        tail = buf[sblk:sblk + pad, :]
        buf[0:pad, :] = tail
        return _silu(acc)

    qbuf[pad:pad + sblk, :] = conv_silu(qx_ref, qbuf, wq_ref)
    kbuf[pad:pad + sblk, :] = conv_silu(kx_ref, kbuf, wk_ref) * (REC_DIM ** -0.5)

    i_bias = bias_ref[0, h]
    f_bias = bias_ref[1, h]
    lane = lax.broadcasted_iota(jnp.int32, (1, LANES), 1)
    sub = lax.broadcasted_iota(jnp.int32, (8, 1), 0)
    ti = lax.broadcasted_iota(jnp.int32, (c, c), 0)
    si = lax.broadcasted_iota(jnp.int32, (c, c), 1)
    causal = si <= ti
    tril = causal.astype(F32)
    nt = (((1,), (1,)), ((), ()))

    def log_sigmoid(z):
        return jnp.minimum(z, 0.0) - jnp.log(1.0 + jnp.exp(-jnp.abs(z)))

    def chunk(ci, carry):
        r0 = pl.multiple_of(ci * c, c)
        rows = pl.ds(r0, c)
        q = qbuf[pl.ds(pad + r0, c), :]
        k = kbuf[pl.ds(pad + r0, c), :]
        v = v_ref[rows, :]
        g = gcol_ref[rows, :]
        i_col = jnp.sum(jnp.where(lane == h, g, 0.0), axis=1, keepdims=True) + i_bias
        f_col = log_sigmoid(jnp.sum(jnp.where(lane == REC_HEADS + h, g, 0.0), axis=1, keepdims=True) + f_bias)
        gr = grow_ref[ci]
        i_row = jnp.sum(jnp.where(sub == h, gr, 0.0), axis=0, keepdims=True) + i_bias
        f_row = log_sigmoid(jnp.sum(jnp.where(sub == REC_HEADS + h, gr, 0.0), axis=0, keepdims=True) + f_bias)

        cum_t = jnp.dot(tril, jnp.broadcast_to(f_col, (c, c)), precision=HIGHEST, preferred_element_type=F32)
        cum_s = lax.dot_general(jnp.broadcast_to(f_row, (8, c)), tril, nt, precision=HIGHEST,
                                preferred_element_type=F32)[0:1, :]
        cum_col = cum_t[:, 0:1]
        m_prev = m_ref[...]
        log_d = jnp.where(causal, cum_t - cum_s + i_row, NEG_BIG)
        log_inter = cum_col + m_prev
        m_t = jnp.maximum(jnp.max(log_d, axis=1, keepdims=True), log_inter)
        w_intra = jnp.exp(log_d - m_t)
        w_inter = jnp.exp(log_inter - m_t)
        qb = q.astype(BF16)
        qk = lax.dot_general(qb, k.astype(BF16), nt, preferred_element_type=F32) * w_intra
        c_st = c_ref[...]
        n_st = n_ref[...]
        num = jnp.dot(qk.astype(BF16), v.astype(BF16), preferred_element_type=F32) \
            + w_inter * jnp.dot(qb, c_st.astype(BF16), preferred_element_type=F32)
        den = jnp.sum(qk, axis=1, keepdims=True) + w_inter * jnp.sum(q * n_st, axis=1, keepdims=True)
        h_out = num / jnp.maximum(jnp.abs(den), jnp.exp(-m_t))
        o_ref[rows, :] = (_sigmoid(og_ref[rows, :]) * h_out).astype(o_ref.dtype)

        last = cum_col[c - 1:c, :]
        log_w = last - cum_col + i_col
        m_new = jnp.maximum(last + m_prev, jnp.max(log_w, axis=0, keepdims=True))
        w_s = jnp.exp(log_w - m_new)
        decay = jnp.exp(last + m_prev - m_new)
        kw = w_s * k
        c_ref[...] = decay * c_st + jnp.dot(kw.T.astype(BF16), v.astype(BF16), preferred_element_type=F32)
        n_ref[...] = decay * n_st + jnp.sum(kw, axis=0, keepdims=True)
        m_ref[...] = m_new
        return carry

    lax.fori_loop(0, sblk // c, chunk, 0)


def _mlstm(proj, gates_t, conv_w, biases, batch, seq, sblk):
    ns = seq // sblk

    def col(off):
        return lambda b, h, s: (b * ns + s, off // LANES + h)

    blk = (sblk, REC_DIM)
    return pl.pallas_call(
        functools.partial(_mlstm_kernel, sblk=sblk),
        grid=(batch, REC_HEADS, ns),
        in_specs=[
            pl.BlockSpec(memory_space=pltpu.SMEM),
            pl.BlockSpec(blk, col(OFF_QKM)),
            pl.BlockSpec(blk, col(OFF_QKM + REC_WIDTH)),
            pl.BlockSpec(blk, col(OFF_VM)),
            pl.BlockSpec(blk, col(OFF_OM)),
            pl.BlockSpec(blk, lambda b, h, s: (b * ns + s, OFF_GATE // LANES)),
            pl.BlockSpec((sblk // CHUNK, 8, CHUNK), lambda b, h, s: (b * ns + s, 0, 0)),
            pl.BlockSpec((MLSTM_CONV, REC_DIM), lambda b, h, s: (0, h)),
            pl.BlockSpec((MLSTM_CONV, REC_DIM), lambda b, h, s: (0, REC_HEADS + h)),
        ],
        out_specs=pl.BlockSpec(blk, lambda b, h, s: (b * ns + s, h)),
        out_shape=jax.ShapeDtypeStruct((batch * seq, REC_WIDTH), BF16),
        scratch_shapes=[
            pltpu.VMEM((sblk + 8, REC_DIM), F32),
            pltpu.VMEM((sblk + 8, REC_DIM), F32),
            pltpu.VMEM((REC_DIM, REC_DIM), F32),
            pltpu.VMEM((1, REC_DIM), F32),
            pltpu.VMEM((1, 1), F32),
        ],
        compiler_params=_cparams("parallel", "parallel", "arbitrary"),
        name="mlstm_scan",
    )(biases, proj, proj, proj, proj, proj, gates_t, conv_w, conv_w)


def _out_proj_kernel(x_ref, ya_ref, yh_ref, ym_ref, wa_ref, wh_ref, wm_ref, o_ref):
    acc = jnp.dot(ya_ref[...], wa_ref[...], preferred_element_type=F32)
    acc = acc + jnp.dot(yh_ref[...], wh_ref[...], preferred_element_type=F32)
    acc = acc + jnp.dot(ym_ref[...], wm_ref[...], preferred_element_type=F32)
    o_ref[...] = x_ref[...] + acc


def _out_proj(x, ya, yh, ym, w_out, *, tm, tn):
    t, d = x.shape
    ka, kh = ya.shape[1], yh.shape[1]
    wa, wh, wm = w_out[:ka], w_out[ka:ka + kh], w_out[ka + kh:]
    return pl.pallas_call(
        _out_proj_kernel,
        grid=(t // tm, d // tn),
        in_specs=[
            pl.BlockSpec((tm, tn), lambda i, j: (i, j)),
            pl.BlockSpec((tm, ka), lambda i, j: (i, 0)),
            pl.BlockSpec((tm, kh), lambda i, j: (i, 0)),
            pl.BlockSpec((tm, ym.shape[1]), lambda i, j: (i, 0)),
            pl.BlockSpec((ka, tn), lambda i, j: (0, j)),
            pl.BlockSpec((kh, tn), lambda i, j: (0, j)),
            pl.BlockSpec((wm.shape[0], tn), lambda i, j: (0, j)),
        ],
        out_specs=pl.BlockSpec((tm, tn), lambda i, j: (i, j)),
        out_shape=jax.ShapeDtypeStruct((t, d), F32),
        compiler_params=_cparams("parallel", "arbitrary"),
        name="out_proj",
    )(x, ya, yh, ym, wa, wh, wm)


def _extract_top(s, payload=None):
    r = s.shape[0]
    rows = lax.broadcasted_iota(jnp.int32, s.shape, 0).astype(F32)
    vals, picks = [], []
    for _ in range(PEER_TOPK):
        m = jnp.max(s, axis=0, keepdims=True)
        i = jnp.min(jnp.where(s == m, rows, float(r)), axis=0, keepdims=True)
        hit = rows == i
        s = jnp.where(hit, -jnp.inf, s)
        vals.append(m)
        picks.append(i if payload is None else jnp.sum(jnp.where(hit, payload, 0.0), axis=0, keepdims=True))
    return jnp.concatenate(vals, axis=0), jnp.concatenate(picks, axis=0)


def _route_kernel(q_ref, sk_ref, idx_ref, gate_ref):
    nt = (((1,), (1,)), ((), ()))
    ids, gates = [], []
    for h in range(PEER_HEADS):
        base = h * 2 * PEER_HALF
        s1 = lax.dot_general(sk_ref[0, h], q_ref[:, base:base + PEER_HALF], nt,
                             precision=HIGHEST, preferred_element_type=F32)
        s2 = lax.dot_general(sk_ref[1, h], q_ref[:, base + PEER_HALF:base + 2 * PEER_HALF], nt,
                             precision=HIGHEST, preferred_element_type=F32)
        v1, i1 = _extract_top(s1)
        v2, i2 = _extract_top(s2)
        cand = jnp.concatenate([v1[a:a + 1] + v2 for a in range(PEER_TOPK)], axis=0)
        cidx = jnp.concatenate([i1[a:a + 1] * PEER_NKEYS + i2 for a in range(PEER_TOPK)], axis=0)
        top_s, eid = _extract_top(cand, cidx)
        p = jnp.exp(top_s - jnp.max(top_s, axis=0, keepdims=True))
        gates.append(p / jnp.sum(p, axis=0, keepdims=True))
        ids.append(eid)
    idx_ref[...] = jnp.concatenate(ids, axis=0).T.astype(jnp.int32)
    gate_ref[...] = jnp.concatenate(gates, axis=0).T


def _route(qv, sub_keys, *, tn):
    t = qv.shape[0]
    return pl.pallas_call(
        _route_kernel,
        grid=(t // tn,),
        in_specs=[
            pl.BlockSpec((tn, qv.shape[1]), lambda i: (i, 0)),
            pl.BlockSpec(sub_keys.shape, lambda i: (0, 0, 0, 0)),
        ],
        out_specs=[pl.BlockSpec((tn, PEER_SLOTS), lambda i: (i, 0))] * 2,
        out_shape=[jax.ShapeDtypeStruct((t, PEER_SLOTS), jnp.int32), jax.ShapeDtypeStruct((t, PEER_SLOTS), F32)],
        compiler_params=_cparams("parallel"),
        name="peer_route",
    )(qv, sub_keys)


PEER_RING = 4
PEER_AHEAD = 3


def _erf(x):
    return lax.erf(x)


def _peer_kernel(idx_ref, x_ref, g_ref, gate_ref, tab_ref, o_ref, buf, sem, *, tb):
    d = D_MODEL

    def row_copy(tok, k, slot):
        e = idx_ref[tok, k]
        return pltpu.make_async_copy(tab_ref.at[pl.ds(e, 1), :], buf.at[slot, pl.ds(k, 1), :], sem.at[slot])

    def issue(tok, slot):
        for k in range(PEER_SLOTS):
            row_copy(tok, k, slot).start()

    def wait(slot):
        pltpu.make_async_copy(tab_ref.at[pl.ds(0, PEER_SLOTS), :], buf.at[slot], sem.at[slot]).wait()

    for t0 in range(PEER_AHEAD):
        issue(t0, t0 % PEER_RING)

    gnorm = g_ref[...]

    def token(t, carry):
        slot = t % PEER_RING

        @pl.when(t + PEER_AHEAD < tb)
        def _():
            issue(t + PEER_AHEAD, (t + PEER_AHEAD) % PEER_RING)

        xr = x_ref[pl.ds(t, 1), :]
        hr = xr * lax.rsqrt(jnp.mean(xr * xr, axis=-1, keepdims=True) + NORM_EPS) * gnorm
        grow = gate_ref[pl.ds(t, 1), :]
        gcol = jnp.broadcast_to(grow, (8, PEER_SLOTS)).T[:, 0:1]
        wait(slot)
        u = buf[slot, :, 0:d]
        pre = jnp.sum(u * hr, axis=1, keepdims=True)
        act = 0.5 * pre * (1.0 + _erf(pre * (2.0 ** -0.5))) * gcol
        v = buf[slot, :, d:2 * d]
        o_ref[pl.ds(t, 1), :] = xr + jnp.sum(v * act, axis=0, keepdims=True)
        return carry

    lax.fori_loop(0, tb, token, 0)


def _peer_experts(x, g, idx, gate, table, *, tb):
    t, d = x.shape
    return pl.pallas_call(
        functools.partial(_peer_kernel, tb=tb),
        grid=(t // tb,),
        in_specs=[
            pl.BlockSpec((tb, PEER_SLOTS), lambda i: (i, 0), memory_space=pltpu.SMEM),
            pl.BlockSpec((tb, d), lambda i: (i, 0)),
            pl.BlockSpec((1, d), lambda i: (0, 0)),
            pl.BlockSpec((tb, PEER_SLOTS), lambda i: (i, 0)),
            pl.BlockSpec(memory_space=pl.ANY),
        ],
        out_specs=pl.BlockSpec((tb, d), lambda i: (i, 0)),
        out_shape=jax.ShapeDtypeStruct((t, d), F32),
        scratch_shapes=[
            pltpu.VMEM((PEER_RING, PEER_SLOTS, 2 * d), F32),
            pltpu.SemaphoreType.DMA((PEER_RING,)),
        ],
        compiler_params=_cparams("arbitrary"),
        name="peer_experts",
    )(idx, x, g.reshape(1, d), gate, table)


def _final_norm_kernel(x_ref, g_ref, o_ref):
    x = x_ref[...]
    o_ref[...] = x * lax.rsqrt(jnp.mean(x * x, axis=-1, keepdims=True) + NORM_EPS) * g_ref[...]


def _final_norm(x, g, *, tm):
    t, d = x.shape
    return pl.pallas_call(
        _final_norm_kernel,
        grid=(t // tm,),
        in_specs=[pl.BlockSpec((tm, d), lambda i: (i, 0)), pl.BlockSpec((1, d), lambda i: (0, 0))],
        out_specs=pl.BlockSpec((tm, d), lambda i: (i, 0)),
        out_shape=jax.ShapeDtypeStruct((t, d), F32),
        compiler_params=_cparams("parallel"),
        name="final_norm",
    )(x, g.reshape(1, d))


def _mixer_layer(x, batch, seq, norm_g, w_in, w_out, sinks, lower_bound, hgrn_gain, conv_w, i_bias, f_bias,
                 cos_t, sin_t, *, tm, sblk):
    w_in_p = jnp.pad(w_in.astype(BF16), ((0, 0), (0, IN_PAD - IN_DIM)))
    proj = _norm_matmul(x, norm_g, w_in_p, tm=tm, tn=512)
    gates_t = proj[:, OFF_GATE:OFF_GATE + 2 * REC_HEADS].reshape(-1, CHUNK, 2 * REC_HEADS).transpose(0, 2, 1)
    ya = _attention(proj, sinks, cos_t, sin_t, batch, seq)
    yh = _hgrn(proj, lower_bound, hgrn_gain, batch, seq, sblk)
    ym = _mlstm(proj, gates_t, conv_w, jnp.stack([i_bias, f_bias]), batch, seq, sblk)
    return _out_proj(x, ya, yh, ym, w_out.astype(BF16), tm=tm, tn=1024)


def _peer_layer(x, norm_g, w_query, sub_keys, expert_u, expert_v, *, tm, tb):
    qv = _norm_matmul(x, norm_g, w_query.astype(BF16), tm=tm, tn=512)
    idx, gate = _route(qv, sub_keys, tn=LANES)
    table = jnp.concatenate([expert_u, expert_v], axis=1)
    return _peer_experts(x, norm_g, idx, gate, table, tb=tb)


def _forward(x, norm_mix, w_in, w_out, attn_sinks, hgrn_lb_logits, hgrn_out_norm, mlstm_conv, mlstm_i_bias,
             mlstm_f_bias, norm_ffn, peer_query, peer_sub_keys, peer_u, peer_v, norm_final, *, tm, sblk, tb):
    batch, seq, d = x.shape
    depth = w_in.shape[0]
    p = jax.nn.softmax(hgrn_lb_logits.astype(F32), axis=0)
    lower_bounds = jnp.cumsum(p, axis=0) - p[0:1]
    cos_t, sin_t = _rope_tables(seq)
    xt = x.reshape(batch * seq, d)
    for l in range(depth):
        xt = _mixer_layer(xt, batch, seq, norm_mix[l], w_in[l], w_out[l], attn_sinks[l], lower_bounds[l],
                          hgrn_out_norm[l], mlstm_conv[l], mlstm_i_bias[l], mlstm_f_bias[l], cos_t, sin_t,
                          tm=tm, sblk=sblk)
        xt = _peer_layer(xt, norm_ffn[l], peer_query[l], peer_sub_keys[l], peer_u[l], peer_v[l], tm=tm, tb=tb)
    return _final_norm(xt, norm_final, tm=tm).reshape(batch, seq, d)


def kernel(x, norm_mix, w_in, w_out, attn_sinks, hgrn_lb_logits, hgrn_out_norm, mlstm_conv, mlstm_i_bias, mlstm_f_bias, norm_ffn, peer_query, peer_sub_keys, peer_u, peer_v, norm_final):
    return _forward(x, norm_mix, w_in, w_out, attn_sinks, hgrn_lb_logits, hgrn_out_norm, mlstm_conv,
                    mlstm_i_bias, mlstm_f_bias, norm_ffn, peer_query, peer_sub_keys, peer_u, peer_v, norm_final,
                    tm=512, sblk=512, tb=128)
```

```python
import functools

import jax
import jax.numpy as jnp
import numpy as np
from jax import lax
from jax.experimental import pallas as pl
from jax.experimental.pallas import tpu as pltpu

F32 = jnp.float32
BF16 = jnp.bfloat16
HIGHEST = lax.Precision.HIGHEST

D_MODEL = 2048
ATTN_HEAD_DIM = 64
ATTN_Q_HEADS = 16
ATTN_KV_HEADS = 2
ATTN_GROUP = ATTN_Q_HEADS // ATTN_KV_HEADS
WINDOW = 128
ROPE_THETA = 10000.0
REC_HEADS = 4
REC_DIM = 128
REC_WIDTH = REC_HEADS * REC_DIM
MLSTM_CONV = 4
CHUNK = 64
PEER_HEADS = 8
PEER_NKEYS = 128
PEER_TOPK = 16
PEER_HALF = 128
PEER_SLOTS = PEER_HEADS * PEER_TOPK
NORM_EPS = 1e-6
NEG_BIG = -1e30
TINY = 1e-30

LANES = 128
SUBLANES = 8
PEER_ROWS = D_MODEL // LANES
VMEM_LIMIT = 48 * 1024 * 1024

OFF_QA, OFF_KA, OFF_VA = 0, 1024, 1152
OFF_QH, OFF_FH, OFF_IH, OFF_GH = 1280, 1792, 2304, 2816
OFF_QKM, OFF_VM, OFF_OM, OFF_GATE = 3328, 4352, 4864, 5376
IN_DIM = 5384
IN_PAD = 5632


def _cparams(*sem):
    return pltpu.CompilerParams(dimension_semantics=sem, vmem_limit_bytes=VMEM_LIMIT)


def _sigmoid(x):
    return 1.0 / (1.0 + jnp.exp(-x))


def _silu(x):
    return x * _sigmoid(x)


def _norm_matmul_kernel(x_ref, g_ref, w_ref, o_ref, h_ref):
    @pl.when(pl.program_id(1) == 0)
    def _():
        x = x_ref[...]
        y = x * lax.rsqrt(jnp.mean(x * x, axis=-1, keepdims=True) + NORM_EPS) * g_ref[...]
        h_ref[...] = y.astype(BF16)

    o_ref[...] = jnp.dot(h_ref[...], w_ref[...], preferred_element_type=F32)


def _norm_matmul(x, g, w, *, tm, tn):
    t, d = x.shape
    n = w.shape[1]
    return pl.pallas_call(
        _norm_matmul_kernel,
        grid=(t // tm, n // tn),
        in_specs=[
            pl.BlockSpec((tm, d), lambda i, j: (i, 0)),
            pl.BlockSpec((1, d), lambda i, j: (0, 0)),
            pl.BlockSpec((d, tn), lambda i, j: (0, j)),
        ],
        out_specs=pl.BlockSpec((tm, tn), lambda i, j: (i, j)),
        out_shape=jax.ShapeDtypeStruct((t, n), F32),
        scratch_shapes=[pltpu.VMEM((tm, d), BF16)],
        compiler_params=_cparams("parallel", "arbitrary"),
        name="norm_matmul",
    )(x, g.reshape(1, d), w)


def _attn_kernel(sink_ref, q_ref, kc_ref, kp_ref, vc_ref, vp_ref, cc_ref, sc_ref, cp_ref, sp_ref, o_ref):
    n = pl.program_id(1)
    w = WINDOW
    lane = lax.broadcasted_iota(jnp.int32, (w, LANES), 1)
    first_half = (lane % ATTN_HEAD_DIM) < (ATTN_HEAD_DIM // 2)

    def rope(x, cos, sin_signed):
        rot = jnp.where(first_half, pltpu.roll(x, LANES - 32, 1), pltpu.roll(x, 32, 1))
        return x * cos + rot * sin_signed

    cos_c, sin_c = cc_ref[...], sc_ref[...]
    k_band = jnp.concatenate(
        [rope(kp_ref[...], cp_ref[...], sp_ref[...]), rope(kc_ref[...], cos_c, sin_c)], axis=0)
    v_band = jnp.concatenate([vp_ref[...], vc_ref[...]], axis=0)
    lane2 = lax.broadcasted_iota(jnp.int32, (2 * w, LANES), 1)
    low2 = lane2 < ATTN_HEAD_DIM

    def dup(x):
        sw = pltpu.roll(x, ATTN_HEAD_DIM, 1)
        return jnp.where(low2, x, sw).astype(BF16), jnp.where(low2, sw, x).astype(BF16)

    k_dup, v_dup = dup(k_band), dup(v_band)

    qi = lax.broadcasted_iota(jnp.int32, (w, 2 * w), 0)
    kj = lax.broadcasted_iota(jnp.int32, (w, 2 * w), 1)
    rel = qi + w - kj
    valid = (rel >= 0) & (rel < WINDOW) & ((kj >= w) | (n > 0))
    low = lane < ATTN_HEAD_DIM

    for c in range(ATTN_Q_HEADS // 2):
        qc = rope(q_ref[:, c * LANES:(c + 1) * LANES], cos_c, sin_c)
        kvh = (2 * c) // ATTN_GROUP
        outs = []
        for j in range(2):
            head = 2 * c + j
            qm = jnp.where(low if j == 0 else ~low, qc, 0.0).astype(BF16)
            logits = lax.dot_general(qm, k_dup[kvh], (((1,), (1,)), ((), ())), preferred_element_type=F32)
            logits = jnp.where(valid, logits * (ATTN_HEAD_DIM ** -0.5), NEG_BIG)
            sink = sink_ref[head]
            m = jnp.maximum(jnp.max(logits, axis=-1, keepdims=True), sink)
            p = jnp.exp(logits - m)
            denom = jnp.sum(p, axis=-1, keepdims=True) + jnp.exp(sink - m)
            o = jnp.dot(p.astype(BF16), v_dup[kvh], preferred_element_type=F32)
            outs.append(o / denom)
        o_ref[:, c * LANES:(c + 1) * LANES] = jnp.where(low, outs[0], outs[1]).astype(o_ref.dtype)


def _attention(proj, sinks, cos_t, sin_t, batch, seq):
    nblk = seq // WINDOW
    kcol, vcol = OFF_KA // LANES, OFF_VA // LANES

    def cur(col):
        return lambda b, n: (b * nblk + n, col)

    def prev(col):
        return lambda b, n: (b * nblk + jnp.maximum(n - 1, 0), col)

    blk = (WINDOW, LANES)
    return pl.pallas_call(
        _attn_kernel,
        grid=(batch, nblk),
        in_specs=[
            pl.BlockSpec(memory_space=pltpu.SMEM),
            pl.BlockSpec((WINDOW, ATTN_Q_HEADS * ATTN_HEAD_DIM), cur(0)),
            pl.BlockSpec(blk, cur(kcol)),
            pl.BlockSpec(blk, prev(kcol)),
            pl.BlockSpec(blk, cur(vcol)),
            pl.BlockSpec(blk, prev(vcol)),
            pl.BlockSpec(blk, lambda b, n: (n, 0)),
            pl.BlockSpec(blk, lambda b, n: (n, 0)),
            pl.BlockSpec(blk, lambda b, n: (jnp.maximum(n - 1, 0), 0)),
            pl.BlockSpec(blk, lambda b, n: (jnp.maximum(n - 1, 0), 0)),
        ],
        out_specs=pl.BlockSpec((WINDOW, ATTN_Q_HEADS * ATTN_HEAD_DIM), cur(0)),
        out_shape=jax.ShapeDtypeStruct((batch * seq, ATTN_Q_HEADS * ATTN_HEAD_DIM), BF16),
        compiler_params=_cparams("parallel", "arbitrary"),
        name="swa_attention",
    )(sinks, proj, proj, proj, proj, proj, cos_t, sin_t, cos_t, sin_t)


def _rope_tables(seq):
    half = ATTN_HEAD_DIM // 2
    inv = ROPE_THETA ** (-jnp.arange(half, dtype=F32) / half)
    ang = jnp.arange(seq, dtype=F32)[:, None] * inv[None, :]
    cos, sin = jnp.cos(ang), jnp.sin(ang)
    return jnp.tile(cos, (1, 4)), jnp.concatenate([-sin, sin, -sin, sin], axis=1)


def _hgrn_constants():
    c = CHUNK
    levels = int(np.log2(c))
    t = np.arange(c)[:, None]
    r = np.arange(c)[None, :]
    dq, dk, masks = [], [], [np.eye(c)]
    for lv in range(1, levels + 1):
        half = 1 << (lv - 1)
        upper = (t & half) != 0
        mid_q = t & ~(half - 1)
        dq.append((upper & (r >= mid_q) & (r <= t)).astype(np.float32))
        mid_k = (t & ~(half - 1)) + half
        dk.append((~upper & (r >= t + 1) & (r <= mid_k - 1)).astype(np.float32))
        same = (t >> lv) == (r >> lv)
        masks.append((same & upper & ((r & half) == 0)).astype(np.float32))
    tril = (r <= t).astype(np.float32)
    rem = (r > t).astype(np.float32)
    dall = np.concatenate(dq + dk + [tril, rem], axis=0)
    return jnp.asarray(dall), jnp.asarray(np.concatenate(masks, axis=0))


def _hgrn_kernel(q_ref, f_ref, i_ref, g_ref, lb_ref, gain_ref, dall_ref, mask_ref, o_ref, st_ref, *, sblk):
    c = CHUNK
    levels = int(np.log2(c))

    @pl.when(pl.program_id(2) == 0)
    def _():
        st_ref[...] = jnp.zeros_like(st_ref)

    lb = lb_ref[...]
    gain = gain_ref[...]
    dall = dall_ref[...]
    nt = (((1,), (1,)), ((), ()))

    def chunk(ci, carry):
        rows = pl.ds(pl.multiple_of(ci * c, c), c)
        fz = f_ref[rows, :]
        f_gate = lb + (1.0 - lb) * _sigmoid(fz)
        a = jnp.log(jnp.maximum(f_gate, TINY))
        k = (1.0 - lb) * _sigmoid(-fz)
        q = _silu(q_ref[rows, :])
        v = i_ref[rows, :]
        e = jnp.exp(jnp.dot(dall, a, precision=HIGHEST, preferred_element_type=F32))
        base = 2 * levels * c
        e_cum = e[base:base + c]
        e_rem = e[base + c:base + 2 * c]

        scores = mask_ref[0:c, :] * lax.dot_general(q.astype(BF16), k.astype(BF16), nt, preferred_element_type=F32)
        for lv in range(levels):
            qd = (q * e[lv * c:(lv + 1) * c]).astype(BF16)
            kd = (k * e[(levels + lv) * c:(levels + lv + 1) * c]).astype(BF16)
            scores = scores + mask_ref[(lv + 1) * c:(lv + 2) * c, :] * lax.dot_general(
                qd, kd, nt, preferred_element_type=F32)

        st = st_ref[...]
        o = jnp.dot(scores.astype(BF16), v.astype(BF16), preferred_element_type=F32)
        o = o + lax.dot_general((q * e_cum).astype(BF16), st.astype(BF16), nt, preferred_element_type=F32)
        k_dec = (k * e_rem).astype(BF16)
        st_ref[...] = st * e_cum[c - 1:c, :] + jnp.dot(v.T.astype(BF16), k_dec, preferred_element_type=F32)

        y = o * lax.rsqrt(jnp.mean(o * o, axis=-1, keepdims=True) + NORM_EPS) * gain
        o_ref[rows, :] = (y * _silu(g_ref[rows, :])).astype(o_ref.dtype)
        return carry

    lax.fori_loop(0, sblk // c, chunk, 0)


def _hgrn(proj, lower_bound, gain, batch, seq, sblk):
    ns = seq // sblk
    dall, masks = _hgrn_constants()

    def col(off):
        return lambda b, h, s: (b * ns + s, off // LANES + h)

    blk = (sblk, REC_DIM)
    vec = pl.BlockSpec((1, REC_DIM), lambda b, h, s: (0, h))
    return pl.pallas_call(
        functools.partial(_hgrn_kernel, sblk=sblk),
        grid=(batch, REC_HEADS, ns),
        in_specs=[
            pl.BlockSpec(blk, col(OFF_QH)),
            pl.BlockSpec(blk, col(OFF_FH)),
            pl.BlockSpec(blk, col(OFF_IH)),
            pl.BlockSpec(blk, col(OFF_GH)),
            vec,
            vec,
            pl.BlockSpec(dall.shape, lambda b, h, s: (0, 0)),
            pl.BlockSpec(masks.shape, lambda b, h, s: (0, 0)),
        ],
        out_specs=pl.BlockSpec(blk, lambda b, h, s: (b * ns + s, h)),
        out_shape=jax.ShapeDtypeStruct((batch * seq, REC_WIDTH), BF16),
        scratch_shapes=[pltpu.VMEM((REC_DIM, REC_DIM), F32)],
        compiler_params=_cparams("parallel", "parallel", "arbitrary"),
        name="hgrn2_scan",
    )(proj, proj, proj, proj, lower_bound.reshape(1, REC_WIDTH), gain.reshape(1, REC_WIDTH), dall, masks)


def _mlstm_kernel(bias_ref, qx_ref, kx_ref, v_ref, og_ref, gcol_ref, grow_ref, wq_ref, wk_ref, o_ref,
                  qbuf, kbuf, c_ref, n_ref, m_ref, *, sblk):
    c = CHUNK
    h = pl.program_id(1)
    pad = 8

    @pl.when(pl.program_id(2) == 0)
    def _():
        qbuf[0:pad, :] = jnp.zeros((pad, REC_DIM), F32)
        kbuf[0:pad, :] = jnp.zeros((pad, REC_DIM), F32)
        c_ref[...] = jnp.zeros_like(c_ref)
        n_ref[...] = jnp.zeros_like(n_ref)
        m_ref[...] = jnp.zeros_like(m_ref)

    def conv_silu(x_ref, buf, w_ref):
        buf[pad:pad + sblk, :] = x_ref[...]
        acc = jnp.zeros((sblk, REC_DIM), F32)
        for j in range(MLSTM_CONV):
            acc = acc + buf[pad - (MLSTM_CONV - 1) + j:pad - (MLSTM_CONV - 1) + j + sblk, :] * w_ref[j:j + 1, :]
        tail = buf[sblk:sblk + pad, :]
        buf[0:pad, :] = tail
        return _silu(acc)

    qbuf[pad:pad + sblk, :] = conv_silu(qx_ref, qbuf, wq_ref)
    kbuf[pad:pad + sblk, :] = conv_silu(kx_ref, kbuf, wk_ref) * (REC_DIM ** -0.5)

    i_bias = bias_ref[0, h]
    f_bias = bias_ref[1, h]
    lane = lax.broadcasted_iota(jnp.int32, (1, LANES), 1)
    sub = lax.broadcasted_iota(jnp.int32, (8, 1), 0)
    ti = lax.broadcasted_iota(jnp.int32, (c, c), 0)
    si = lax.broadcasted_iota(jnp.int32, (c, c), 1)
    causal = si <= ti
    tril = causal.astype(F32)
    nt = (((1,), (1,)), ((), ()))

    def log_sigmoid(z):
        return jnp.minimum(z, 0.0) - jnp.log(1.0 + jnp.exp(-jnp.abs(z)))

    def chunk(ci, carry):
        r0 = pl.multiple_of(ci * c, c)
        rows = pl.ds(r0, c)
        q = qbuf[pl.ds(pad + r0, c), :]
        k = kbuf[pl.ds(pad + r0, c), :]
        v = v_ref[rows, :]
        g = gcol_ref[rows, :]
        i_col = jnp.sum(jnp.where(lane == h, g, 0.0), axis=1, keepdims=True) + i_bias
        f_col = log_sigmoid(jnp.sum(jnp.where(lane == REC_HEADS + h, g, 0.0), axis=1, keepdims=True) + f_bias)
        gr = grow_ref[ci]
        i_row = jnp.sum(jnp.where(sub == h, gr, 0.0), axis=0, keepdims=True) + i_bias
        f_row = log_sigmoid(jnp.sum(jnp.where(sub == REC_HEADS + h, gr, 0.0), axis=0, keepdims=True) + f_bias)

        cum_t = jnp.dot(tril, jnp.broadcast_to(f_col, (c, c)), precision=HIGHEST, preferred_element_type=F32)
        cum_s = lax.dot_general(jnp.broadcast_to(f_row, (8, c)), tril, nt, precision=HIGHEST,
                                preferred_element_type=F32)[0:1, :]
        cum_col = cum_t[:, 0:1]
        m_prev = m_ref[...]
        log_d = jnp.where(causal, cum_t - cum_s + i_row, NEG_BIG)
        log_inter = cum_col + m_prev
        m_t = jnp.maximum(jnp.max(log_d, axis=1, keepdims=True), log_inter)
        w_intra = jnp.exp(log_d - m_t)
        w_inter = jnp.exp(log_inter - m_t)
        qb = q.astype(BF16)
        qk = lax.dot_general(qb, k.astype(BF16), nt, preferred_element_type=F32) * w_intra
        c_st = c_ref[...]
        n_st = n_ref[...]
        num = jnp.dot(qk.astype(BF16), v.astype(BF16), preferred_element_type=F32) \
            + w_inter * jnp.dot(qb, c_st.astype(BF16), preferred_element_type=F32)
        den = jnp.sum(qk, axis=1, keepdims=True) + w_inter * jnp.sum(q * n_st, axis=1, keepdims=True)
        h_out = num / jnp.maximum(jnp.abs(den), jnp.exp(-m_t))
        o_ref[rows, :] = (_sigmoid(og_ref[rows, :]) * h_out).astype(o_ref.dtype)

        last = cum_col[c - 1:c, :]
        log_w = last - cum_col + i_col
        m_new = jnp.maximum(last + m_prev, jnp.max(log_w, axis=0, keepdims=True))
        w_s = jnp.exp(log_w - m_new)
        decay = jnp.exp(last + m_prev - m_new)
        kw = w_s * k
        c_ref[...] = decay * c_st + jnp.dot(kw.T.astype(BF16), v.astype(BF16), preferred_element_type=F32)
        n_ref[...] = decay * n_st + jnp.sum(kw, axis=0, keepdims=True)
        m_ref[...] = m_new
        return carry

    lax.fori_loop(0, sblk // c, chunk, 0)


def _mlstm(proj, gates_t, conv_w, biases, batch, seq, sblk):
    ns = seq // sblk

    def col(off):
        return lambda b, h, s: (b * ns + s, off // LANES + h)

    blk = (sblk, REC_DIM)
    return pl.pallas_call(
        functools.partial(_mlstm_kernel, sblk=sblk),
        grid=(batch, REC_HEADS, ns),
        in_specs=[
            pl.BlockSpec(memory_space=pltpu.SMEM),
            pl.BlockSpec(blk, col(OFF_QKM)),
            pl.BlockSpec(blk, col(OFF_QKM + REC_WIDTH)),
            pl.BlockSpec(blk, col(OFF_VM)),
            pl.BlockSpec(blk, col(OFF_OM)),
            pl.BlockSpec(blk, lambda b, h, s: (b * ns + s, OFF_GATE // LANES)),
            pl.BlockSpec((sblk // CHUNK, 8, CHUNK), lambda b, h, s: (b * ns + s, 0, 0)),
            pl.BlockSpec((MLSTM_CONV, REC_DIM), lambda b, h, s: (0, h)),
            pl.BlockSpec((MLSTM_CONV, REC_DIM), lambda b, h, s: (0, REC_HEADS + h)),
        ],
        out_specs=pl.BlockSpec(blk, lambda b, h, s: (b * ns + s, h)),
        out_shape=jax.ShapeDtypeStruct((batch * seq, REC_WIDTH), BF16),
        scratch_shapes=[
            pltpu.VMEM((sblk + 8, REC_DIM), F32),
            pltpu.VMEM((sblk + 8, REC_DIM), F32),
            pltpu.VMEM((REC_DIM, REC_DIM), F32),
            pltpu.VMEM((1, REC_DIM), F32),
            pltpu.VMEM((1, 1), F32),
        ],
        compiler_params=_cparams("parallel", "parallel", "arbitrary"),
        name="mlstm_scan",
    )(biases, proj, proj, proj, proj, proj, gates_t, conv_w, conv_w)


def _out_proj_kernel(x_ref, ya_ref, yh_ref, ym_ref, wa_ref, wh_ref, wm_ref, o_ref):
    acc = jnp.dot(ya_ref[...], wa_ref[...], preferred_element_type=F32)
    acc = acc + jnp.dot(yh_ref[...], wh_ref[...], preferred_element_type=F32)
    acc = acc + jnp.dot(ym_ref[...], wm_ref[...], preferred_element_type=F32)
    o_ref[...] = x_ref[...] + acc


def _out_proj(x, ya, yh, ym, w_out, *, tm, tn):
    t, d = x.shape
    ka, kh = ya.shape[1], yh.shape[1]
    wa, wh, wm = w_out[:ka], w_out[ka:ka + kh], w_out[ka + kh:]
    return pl.pallas_call(
        _out_proj_kernel,
        grid=(t // tm, d // tn),
        in_specs=[
            pl.BlockSpec((tm, tn), lambda i, j: (i, j)),
            pl.BlockSpec((tm, ka), lambda i, j: (i, 0)),
            pl.BlockSpec((tm, kh), lambda i, j: (i, 0)),
            pl.BlockSpec((tm, ym.shape[1]), lambda i, j: (i, 0)),
            pl.BlockSpec((ka, tn), lambda i, j: (0, j)),
            pl.BlockSpec((kh, tn), lambda i, j: (0, j)),
            pl.BlockSpec((wm.shape[0], tn), lambda i, j: (0, j)),
        ],
        out_specs=pl.BlockSpec((tm, tn), lambda i, j: (i, j)),
        out_shape=jax.ShapeDtypeStruct((t, d), F32),
        compiler_params=_cparams("parallel", "arbitrary"),
        name="out_proj",
    )(x, ya, yh, ym, wa, wh, wm)


def _extract_top(s, payload=None):
    r = s.shape[0]
    rows = lax.broadcasted_iota(jnp.int32, s.shape, 0).astype(F32)
    vals, picks = [], []
    for _ in range(PEER_TOPK):
        m = jnp.max(s, axis=0, keepdims=True)
        i = jnp.min(jnp.where(s == m, rows, float(r)), axis=0, keepdims=True)
        hit = rows == i
        s = jnp.where(hit, -jnp.inf, s)
        vals.append(m)
        picks.append(i if payload is None else jnp.sum(jnp.where(hit, payload, 0.0), axis=0, keepdims=True))
    return jnp.concatenate(vals, axis=0), jnp.concatenate(picks, axis=0)


def _route_kernel(q_ref, sk_ref, idx_ref, gate_ref):
    nt = (((1,), (1,)), ((), ()))
    ids, gates = [], []
    for h in range(PEER_HEADS):
        base = h * 2 * PEER_HALF
        s1 = lax.dot_general(sk_ref[0, h], q_ref[:, base:base + PEER_HALF], nt,
                             precision=HIGHEST, preferred_element_type=F32)
        s2 = lax.dot_general(sk_ref[1, h], q_ref[:, base + PEER_HALF:base + 2 * PEER_HALF], nt,
                             precision=HIGHEST, preferred_element_type=F32)
        v1, i1 = _extract_top(s1)
        v2, i2 = _extract_top(s2)
        sub = lax.broadcasted_iota(jnp.int32, (SUBLANES, v2.shape[1]), 0)
        cands = [v1[0:1] + v2]
        cidxs = [(i1[0:1] * PEER_NKEYS + i2) * PEER_ROWS]
        for a in range(1, PEER_TOPK):
            cands.append(jnp.where(sub < PEER_TOPK // (a + 1), v1[a:a + 1] + v2[0:SUBLANES], -jnp.inf))
            cidxs.append((i1[a:a + 1] * PEER_NKEYS + i2[0:SUBLANES]) * PEER_ROWS)
        top_s, eid = _extract_top(jnp.concatenate(cands, axis=0), jnp.concatenate(cidxs, axis=0))
        p = jnp.exp(top_s - jnp.max(top_s, axis=0, keepdims=True))
        gates.append(p / jnp.sum(p, axis=0, keepdims=True))
        ids.append(eid)
    idx_ref[...] = jnp.concatenate(ids, axis=0).T.astype(jnp.int32)
    gate_ref[...] = jnp.concatenate(gates, axis=0).T


def _route(qv, sub_keys, *, tn):
    t = qv.shape[0]
    return pl.pallas_call(
        _route_kernel,
        grid=(t // tn,),
        in_specs=[
            pl.BlockSpec((tn, qv.shape[1]), lambda i: (i, 0)),
            pl.BlockSpec(sub_keys.shape, lambda i: (0, 0, 0, 0)),
        ],
        out_specs=[pl.BlockSpec((tn, PEER_SLOTS), lambda i: (i, 0))] * 2,
        out_shape=[jax.ShapeDtypeStruct((t, PEER_SLOTS), jnp.int32), jax.ShapeDtypeStruct((t, PEER_SLOTS), F32)],
        compiler_params=_cparams("parallel"),
        name="peer_route",
    )(qv, sub_keys)


PEER_RING = 4
PEER_STRIDE = PEER_ROWS + 1
PEER_COLS = D_MODEL // LANES
PEER_GROUPS = PEER_SLOTS // SUBLANES


def _peer_table(u, v):
    e = u.shape[0]
    ub = lax.bitcast_convert_type(u.astype(BF16), jnp.uint16).astype(jnp.uint32)
    vb = lax.bitcast_convert_type(v.astype(BF16), jnp.uint16).astype(jnp.uint32)
    return (ub | (vb << 16)).reshape(e * PEER_ROWS, LANES)


def _peer_kernel(idx_ref, x_ref, g_ref, gate_ref, tab_ref, o_ref, *scratch, tb):
    bufs, sem = scratch[:PEER_RING], scratch[PEER_RING]
    ahead = PEER_RING - 1
    gnorm = g_ref[...]

    def issue(tok, j):
        for k in range(PEER_SLOTS):
            row0 = pl.multiple_of(idx_ref[tok, k], PEER_ROWS)
            pltpu.make_async_copy(tab_ref.at[pl.ds(row0, PEER_ROWS), :],
                                  bufs[j].at[pl.ds(k * PEER_STRIDE, PEER_ROWS), :], sem.at[j]).start()

    def wait(j):
        n = PEER_SLOTS * PEER_ROWS
        pltpu.make_async_copy(tab_ref.at[pl.ds(0, n), :], bufs[j].at[pl.ds(0, n), :], sem.at[j]).wait()

    def words(j, c):
        return bufs[j][pl.ds(c, PEER_SLOTS, stride=PEER_STRIDE), :]

    def compute(tok, j):
        xr = x_ref[pl.ds(tok, 1), :]
        hr = xr * lax.rsqrt(jnp.mean(xr * xr, axis=-1, keepdims=True) + NORM_EPS) * gnorm
        grow = gate_ref[pl.ds(tok, 1), :]
        gcol = jnp.broadcast_to(grow, (SUBLANES, PEER_SLOTS)).T[:, 0:1]
        acc = None
        for c in range(PEER_COLS):
            u = lax.bitcast_convert_type(words(j, c) << 16, F32)
            term = u * hr[:, c * LANES:(c + 1) * LANES]
            acc = term if acc is None else acc + term
        pre = jnp.sum(acc, axis=1, keepdims=True)
        act = 0.5 * pre * (1.0 + lax.erf(pre * (2.0 ** -0.5))) * gcol
        outs = []
        for c in range(PEER_COLS):
            v = lax.bitcast_convert_type(words(j, c) & jnp.uint32(0xFFFF0000), F32)
            outs.append(jnp.sum(v * act, axis=0, keepdims=True))
        o_ref[pl.ds(tok, 1), :] = xr + jnp.concatenate(outs, axis=1)

    for t0 in range(ahead):
        issue(t0, t0)

    def group(i, carry):
        for j in range(PEER_RING):
            tok = i * PEER_RING + j
            wait(j)
            issue(jnp.minimum(tok + ahead, tb - 1), (j + ahead) % PEER_RING)
            compute(tok, j)
        return carry

    lax.fori_loop(0, tb // PEER_RING, group, 0)
    for j in range(ahead):
        wait(j)


def _peer_experts(x, g, idx, gate, table, *, tb):
    t, d = x.shape
    assert tb % PEER_RING == 0 and t % tb == 0
    return pl.pallas_call(
        functools.partial(_peer_kernel, tb=tb),
        grid=(t // tb,),
        in_specs=[
            pl.BlockSpec((tb, PEER_SLOTS), lambda i: (i, 0), memory_space=pltpu.SMEM),
            pl.BlockSpec((tb, d), lambda i: (i, 0)),
            pl.BlockSpec((1, d), lambda i: (0, 0)),
            pl.BlockSpec((tb, PEER_SLOTS), lambda i: (i, 0)),
            pl.BlockSpec(memory_space=pl.ANY),
        ],
        out_specs=pl.BlockSpec((tb, d), lambda i: (i, 0)),
        out_shape=jax.ShapeDtypeStruct((t, d), F32),
        scratch_shapes=[pltpu.VMEM((PEER_SLOTS * PEER_STRIDE, LANES), jnp.uint32) for _ in range(PEER_RING)]
        + [pltpu.SemaphoreType.DMA((PEER_RING,))],
        compiler_params=_cparams("arbitrary"),
        name="peer_experts",
    )(idx, x, g.reshape(1, d), gate, table)


def _final_norm_kernel(x_ref, g_ref, o_ref):
    x = x_ref[...]
    o_ref[...] = x * lax.rsqrt(jnp.mean(x * x, axis=-1, keepdims=True) + NORM_EPS) * g_ref[...]


def _final_norm(x, g, *, tm):
    t, d = x.shape
    return pl.pallas_call(
        _final_norm_kernel,
        grid=(t // tm,),
        in_specs=[pl.BlockSpec((tm, d), lambda i: (i, 0)), pl.BlockSpec((1, d), lambda i: (0, 0))],
        out_specs=pl.BlockSpec((tm, d), lambda i: (i, 0)),
        out_shape=jax.ShapeDtypeStruct((t, d), F32),
        compiler_params=_cparams("parallel"),
        name="final_norm",
    )(x, g.reshape(1, d))


def _mixer_layer(x, batch, seq, norm_g, w_in, w_out, sinks, lower_bound, hgrn_gain, conv_w, i_bias, f_bias,
                 cos_t, sin_t, *, tm, sblk):
    w_in_p = jnp.pad(w_in.astype(BF16), ((0, 0), (0, IN_PAD - IN_DIM)))
    proj = _norm_matmul(x, norm_g, w_in_p, tm=tm, tn=IN_PAD // 4)
    gates_t = proj[:, OFF_GATE:OFF_GATE + 2 * REC_HEADS].reshape(-1, CHUNK, 2 * REC_HEADS).transpose(0, 2, 1)
    ya = _attention(proj, sinks, cos_t, sin_t, batch, seq)
    yh = _hgrn(proj, lower_bound, hgrn_gain, batch, seq, sblk)
    ym = _mlstm(proj, gates_t, conv_w, jnp.stack([i_bias, f_bias]), batch, seq, sblk)
    return _out_proj(x, ya, yh, ym, w_out.astype(BF16), tm=tm, tn=1024)


def _peer_layer(x, norm_g, w_query, sub_keys, expert_u, expert_v, *, tm, tb):
    qv = _norm_matmul(x, norm_g, w_query.astype(BF16), tm=tm, tn=1024)
    idx, gate = _route(qv, sub_keys, tn=LANES)
    return _peer_experts(x, norm_g, idx, gate, _peer_table(expert_u, expert_v), tb=tb)


def _forward(x, norm_mix, w_in, w_out, attn_sinks, hgrn_lb_logits, hgrn_out_norm, mlstm_conv, mlstm_i_bias,
             mlstm_f_bias, norm_ffn, peer_query, peer_sub_keys, peer_u, peer_v, norm_final, *, tm, sblk, tb):
    batch, seq, d = x.shape
    depth = w_in.shape[0]
    p = jax.nn.softmax(hgrn_lb_logits.astype(F32), axis=0)
    lower_bounds = jnp.cumsum(p, axis=0) - p[0:1]
    cos_t, sin_t = _rope_tables(seq)
    xt = x.reshape(batch * seq, d)
    for l in range(depth):
        xt = _mixer_layer(xt, batch, seq, norm_mix[l], w_in[l], w_out[l], attn_sinks[l], lower_bounds[l],
                          hgrn_out_norm[l], mlstm_conv[l], mlstm_i_bias[l], mlstm_f_bias[l], cos_t, sin_t,
                          tm=tm, sblk=sblk)
        xt = _peer_layer(xt, norm_ffn[l], peer_query[l], peer_sub_keys[l], peer_u[l], peer_v[l], tm=tm, tb=tb)
    return _final_norm(xt, norm_final, tm=tm).reshape(batch, seq, d)


def kernel(x, norm_mix, w_in, w_out, attn_sinks, hgrn_lb_logits, hgrn_out_norm, mlstm_conv, mlstm_i_bias, mlstm_f_bias, norm_ffn, peer_query, peer_sub_keys, peer_u, peer_v, norm_final):
    return _forward(x, norm_mix, w_in, w_out, attn_sinks, hgrn_lb_logits, hgrn_out_norm, mlstm_conv,
                    mlstm_i_bias, mlstm_f_bias, norm_ffn, peer_query, peer_sub_keys, peer_u, peer_v, norm_final,
                    tm=512, sblk=512, tb=256)
```

```python
import functools

import jax
import jax.numpy as jnp
import numpy as np
from jax import lax
from jax.experimental import pallas as pl
from jax.experimental.pallas import tpu as pltpu

F32 = jnp.float32
BF16 = jnp.bfloat16
HIGHEST = lax.Precision.HIGHEST

D_MODEL = 2048
ATTN_HEAD_DIM = 64
ATTN_Q_HEADS = 16
ATTN_KV_HEADS = 2
ATTN_GROUP = ATTN_Q_HEADS // ATTN_KV_HEADS
WINDOW = 128
ROPE_THETA = 10000.0
REC_HEADS = 4
REC_DIM = 128
REC_WIDTH = REC_HEADS * REC_DIM
MLSTM_CONV = 4
CHUNK = 64
PEER_HEADS = 8
PEER_NKEYS = 128
PEER_TOPK = 16
PEER_HALF = 128
PEER_SLOTS = PEER_HEADS * PEER_TOPK
NORM_EPS = 1e-6
NEG_BIG = -1e30
TINY = 1e-30

LANES = 128
SUBLANES = 8
PEER_ROWS = D_MODEL // LANES
VMEM_LIMIT = 48 * 1024 * 1024

OFF_QA = 0
OFF_QH, OFF_FH, OFF_IH, OFF_GH = 1024, 1536, 2048, 2560
OFF_QKM, OFF_VM, OFF_OM = 3072, 4096, 4608
OFF_KA, OFF_VA, OFF_GATE = 5120, 5248, 5376
IN_DIM = 5384
IN_PAD = 5632
_IN_GROUPS = ((0, 1024), (1280, 1792), (1792, 2304), (2304, 2816), (2816, 3328), (3328, 4352), (4352, 4864),
              (4864, 5376), (1024, 1152), (1152, 1280), (5376, 5384))


def _cparams(*sem):
    return pltpu.CompilerParams(dimension_semantics=sem, vmem_limit_bytes=VMEM_LIMIT)


def _sigmoid(x):
    return 1.0 / (1.0 + jnp.exp(-x))


def _silu(x):
    return x * _sigmoid(x)


def _norm_matmul_kernel(x_ref, g_ref, w_ref, o_ref, h_ref):
    @pl.when(pl.program_id(1) == 0)
    def _():
        x = x_ref[...]
        y = x * lax.rsqrt(jnp.mean(x * x, axis=-1, keepdims=True) + NORM_EPS) * g_ref[...]
        h_ref[...] = y.astype(BF16)

    o_ref[...] = jnp.dot(h_ref[...], w_ref[...], preferred_element_type=F32)


def _norm_matmul(x, g, w, *, tm, tn):
    t, d = x.shape
    n = w.shape[1]
    return pl.pallas_call(
        _norm_matmul_kernel,
        grid=(t // tm, n // tn),
        in_specs=[
            pl.BlockSpec((tm, d), lambda i, j: (i, 0)),
            pl.BlockSpec((1, d), lambda i, j: (0, 0)),
            pl.BlockSpec((d, tn), lambda i, j: (0, j)),
        ],
        out_specs=pl.BlockSpec((tm, tn), lambda i, j: (i, j)),
        out_shape=jax.ShapeDtypeStruct((t, n), F32),
        scratch_shapes=[pltpu.VMEM((tm, d), BF16)],
        compiler_params=_cparams("parallel", "arbitrary"),
        name="norm_matmul",
    )(x, g.reshape(1, d), w)


def _attn_kernel(sink_ref, q_ref, kc_ref, kp_ref, vc_ref, vp_ref, cc_ref, sc_ref, cp_ref, sp_ref, o_ref):
    n = pl.program_id(1)
    w = WINDOW
    lane = lax.broadcasted_iota(jnp.int32, (w, LANES), 1)
    first_half = (lane % ATTN_HEAD_DIM) < (ATTN_HEAD_DIM // 2)

    def rope(x, cos, sin_signed):
        rot = jnp.where(first_half, pltpu.roll(x, LANES - 32, 1), pltpu.roll(x, 32, 1))
        return x * cos + rot * sin_signed

    cos_c, sin_c = cc_ref[...], sc_ref[...]
    k_band = jnp.concatenate(
        [rope(kp_ref[...], cp_ref[...], sp_ref[...]), rope(kc_ref[...], cos_c, sin_c)], axis=0)
    v_band = jnp.concatenate([vp_ref[...], vc_ref[...]], axis=0)
    lane2 = lax.broadcasted_iota(jnp.int32, (2 * w, LANES), 1)
    low2 = lane2 < ATTN_HEAD_DIM

    def dup(x):
        sw = pltpu.roll(x, ATTN_HEAD_DIM, 1)
        return jnp.where(low2, x, sw).astype(BF16), jnp.where(low2, sw, x).astype(BF16)

    k_dup, v_dup = dup(k_band), dup(v_band)

    qi = lax.broadcasted_iota(jnp.int32, (w, 2 * w), 0)
    kj = lax.broadcasted_iota(jnp.int32, (w, 2 * w), 1)
    rel = qi + w - kj
    valid = (rel >= 0) & (rel < WINDOW) & ((kj >= w) | (n > 0))
    low = lane < ATTN_HEAD_DIM

    for c in range(ATTN_Q_HEADS // 2):
        qc = rope(q_ref[:, c * LANES:(c + 1) * LANES], cos_c, sin_c)
        kvh = (2 * c) // ATTN_GROUP
        outs = []
        for j in range(2):
            head = 2 * c + j
            qm = jnp.where(low if j == 0 else ~low, qc, 0.0).astype(BF16)
            logits = lax.dot_general(qm, k_dup[kvh], (((1,), (1,)), ((), ())), preferred_element_type=F32)
            logits = jnp.where(valid, logits * (ATTN_HEAD_DIM ** -0.5), NEG_BIG)
            sink = sink_ref[head]
            m = jnp.maximum(jnp.max(logits, axis=-1, keepdims=True), sink)
            p = jnp.exp(logits - m)
            denom = jnp.sum(p, axis=-1, keepdims=True) + jnp.exp(sink - m)
            o = jnp.dot(p.astype(BF16), v_dup[kvh], preferred_element_type=F32)
            outs.append(o / denom)
        o_ref[:, c * LANES:(c + 1) * LANES] = jnp.where(low, outs[0], outs[1]).astype(o_ref.dtype)


def _attention(proj, sinks, cos_t, sin_t, batch, seq):
    nblk = seq // WINDOW
    kcol, vcol = OFF_KA // LANES, OFF_VA // LANES

    def cur(col):
        return lambda b, n: (b * nblk + n, col)

    def prev(col):
        return lambda b, n: (b * nblk + jnp.maximum(n - 1, 0), col)

    blk = (WINDOW, LANES)
    return pl.pallas_call(
        _attn_kernel,
        grid=(batch, nblk),
        in_specs=[
            pl.BlockSpec(memory_space=pltpu.SMEM),
            pl.BlockSpec((WINDOW, ATTN_Q_HEADS * ATTN_HEAD_DIM), cur(0)),
            pl.BlockSpec(blk, cur(kcol)),
            pl.BlockSpec(blk, prev(kcol)),
            pl.BlockSpec(blk, cur(vcol)),
            pl.BlockSpec(blk, prev(vcol)),
            pl.BlockSpec(blk, lambda b, n: (n, 0)),
            pl.BlockSpec(blk, lambda b, n: (n, 0)),
            pl.BlockSpec(blk, lambda b, n: (jnp.maximum(n - 1, 0), 0)),
            pl.BlockSpec(blk, lambda b, n: (jnp.maximum(n - 1, 0), 0)),
        ],
        out_specs=pl.BlockSpec((WINDOW, ATTN_Q_HEADS * ATTN_HEAD_DIM), cur(0)),
        out_shape=jax.ShapeDtypeStruct((batch * seq, ATTN_Q_HEADS * ATTN_HEAD_DIM), BF16),
        compiler_params=_cparams("parallel", "arbitrary"),
        name="swa_attention",
    )(sinks, proj, proj, proj, proj, proj, cos_t, sin_t, cos_t, sin_t)


def _rope_tables(seq):
    half = ATTN_HEAD_DIM // 2
    inv = ROPE_THETA ** (-jnp.arange(half, dtype=F32) / half)
    ang = jnp.arange(seq, dtype=F32)[:, None] * inv[None, :]
    cos, sin = jnp.cos(ang), jnp.sin(ang)
    return jnp.tile(cos, (1, 4)), jnp.concatenate([-sin, sin, -sin, sin], axis=1)


def _hgrn_constants():
    c = CHUNK
    levels = int(np.log2(c))
    t = np.arange(c)[:, None]
    r = np.arange(c)[None, :]
    dq, dk, masks = [], [], [np.eye(c)]
    for lv in range(1, levels + 1):
        half = 1 << (lv - 1)
        upper = (t & half) != 0
        mid_q = t & ~(half - 1)
        dq.append((upper & (r >= mid_q) & (r <= t)).astype(np.float32))
        mid_k = (t & ~(half - 1)) + half
        dk.append((~upper & (r >= t + 1) & (r <= mid_k - 1)).astype(np.float32))
        same = (t >> lv) == (r >> lv)
        masks.append((same & upper & ((r & half) == 0)).astype(np.float32))
    tril = (r <= t).astype(np.float32)
    rem = (r > t).astype(np.float32)
    dall = np.concatenate(dq + dk + [tril, rem], axis=0)
    return jnp.asarray(dall), jnp.asarray(np.concatenate(masks, axis=0))


def _hgrn_kernel(q_ref, f_ref, i_ref, g_ref, lb_ref, gain_ref, dall_ref, mask_ref, o_ref, st_ref, *, sblk):
    c = CHUNK
    levels = int(np.log2(c))

    @pl.when(pl.program_id(1) == 0)
    def _():
        st_ref[...] = jnp.zeros_like(st_ref)

    lb = lb_ref[...]
    gain = gain_ref[...]
    dall = dall_ref[...]
    nt = (((1,), (1,)), ((), ()))

    def chunk(ci, carry):
        rows = pl.ds(pl.multiple_of(ci * c, c), c)
        fz = f_ref[rows, :]
        f_gate = lb + (1.0 - lb) * _sigmoid(fz)
        a = jnp.log(jnp.maximum(f_gate, TINY))
        k_all = (1.0 - lb) * _sigmoid(-fz)
        q_all = _silu(q_ref[rows, :])
        v_all = i_ref[rows, :]
        a_hi, a_mid, a_lo = _split3(a)
        e_all = jnp.exp(jnp.dot(dall, a_hi, preferred_element_type=F32)
                        + jnp.dot(dall, a_mid, preferred_element_type=F32)
                        + jnp.dot(dall, a_lo, preferred_element_type=F32))
        base = 2 * levels * c
        outs = []
        for hh in range(REC_HEADS):
            cols = slice(hh * REC_DIM, (hh + 1) * REC_DIM)
            q, k, v, e = q_all[:, cols], k_all[:, cols], v_all[:, cols], e_all[:, cols]
            e_cum = e[base:base + c]
            e_rem = e[base + c:base + 2 * c]
            scores = mask_ref[0:c, :] * lax.dot_general(q.astype(BF16), k.astype(BF16), nt,
                                                        preferred_element_type=F32)
            for lv in range(levels):
                qd = (q * e[lv * c:(lv + 1) * c]).astype(BF16)
                kd = (k * e[(levels + lv) * c:(levels + lv + 1) * c]).astype(BF16)
                scores = scores + mask_ref[(lv + 1) * c:(lv + 2) * c, :] * lax.dot_general(
                    qd, kd, nt, preferred_element_type=F32)

            st = st_ref[hh]
            o = jnp.dot(scores.astype(BF16), v.astype(BF16), preferred_element_type=F32)
            o = o + lax.dot_general((q * e_cum).astype(BF16), st.astype(BF16), nt, preferred_element_type=F32)
            k_dec = (k * e_rem).astype(BF16)
            st_ref[hh] = st * e_cum[c - 1:c, :] + jnp.dot(v.T.astype(BF16), k_dec, preferred_element_type=F32)
            outs.append(o * lax.rsqrt(jnp.mean(o * o, axis=-1, keepdims=True) + NORM_EPS))
        y = jnp.concatenate(outs, axis=1) * gain
        o_ref[rows, :] = (y * _silu(g_ref[rows, :])).astype(o_ref.dtype)
        return carry

    lax.fori_loop(0, sblk // c, chunk, 0)


def _hgrn(proj, lower_bound, gain, batch, seq, sblk):
    ns = seq // sblk
    dall, masks = _hgrn_constants()

    def col(off):
        return lambda b, s: (b * ns + s, off // REC_WIDTH)

    blk = (sblk, REC_WIDTH)
    vec = pl.BlockSpec((1, REC_WIDTH), lambda b, s: (0, 0))
    return pl.pallas_call(
        functools.partial(_hgrn_kernel, sblk=sblk),
        grid=(batch, ns),
        in_specs=[
            pl.BlockSpec(blk, col(OFF_QH)),
            pl.BlockSpec(blk, col(OFF_FH)),
            pl.BlockSpec(blk, col(OFF_IH)),
            pl.BlockSpec(blk, col(OFF_GH)),
            vec,
            vec,
            pl.BlockSpec(dall.shape, lambda b, s: (0, 0)),
            pl.BlockSpec(masks.shape, lambda b, s: (0, 0)),
        ],
        out_specs=pl.BlockSpec(blk, lambda b, s: (b * ns + s, 0)),
        out_shape=jax.ShapeDtypeStruct((batch * seq, REC_WIDTH), BF16),
        scratch_shapes=[pltpu.VMEM((REC_HEADS, REC_DIM, REC_DIM), F32)],
        compiler_params=_cparams("parallel", "arbitrary"),
        name="hgrn2_scan",
    )(proj, proj, proj, proj, lower_bound.reshape(1, REC_WIDTH), gain.reshape(1, REC_WIDTH),
      dall.astype(BF16), masks)


def _split3(x):
    hi = x.astype(BF16)
    r = x - hi.astype(F32)
    mid = r.astype(BF16)
    return hi, mid, (r - mid.astype(F32)).astype(BF16)


def _mlstm_kernel(bcol_ref, brow_ref, qx_ref, kx_ref, v_ref, og_ref, gcol_ref, grow_ref, wq_ref, wk_ref, o_ref,
                  qbuf, kbuf, c_ref, n_ref, m_ref, *, sblk):
    c = CHUNK
    pad = 8

    @pl.when(pl.program_id(1) == 0)
    def _():
        qbuf[0:pad, :] = jnp.zeros((pad, REC_WIDTH), F32)
        kbuf[0:pad, :] = jnp.zeros((pad, REC_WIDTH), F32)
        c_ref[...] = jnp.zeros_like(c_ref)
        n_ref[...] = jnp.zeros_like(n_ref)
        m_ref[...] = jnp.zeros_like(m_ref)

    def conv_silu(x_ref, buf, w_ref):
        buf[pad:pad + sblk, :] = x_ref[...]
        acc = jnp.zeros((sblk, REC_WIDTH), F32)
        for j in range(MLSTM_CONV):
            acc = acc + buf[pad - (MLSTM_CONV - 1) + j:pad - (MLSTM_CONV - 1) + j + sblk, :] * w_ref[j:j + 1, :]
        tail = buf[sblk:sblk + pad, :]
        buf[0:pad, :] = tail
        return _silu(acc)

    qbuf[pad:pad + sblk, :] = conv_silu(qx_ref, qbuf, wq_ref)
    kbuf[pad:pad + sblk, :] = conv_silu(kx_ref, kbuf, wk_ref) * (REC_DIM ** -0.5)

    ti = lax.broadcasted_iota(jnp.int32, (c, c), 0)
    si = lax.broadcasted_iota(jnp.int32, (c, c), 1)
    causal = si <= ti
    tril = causal.astype(BF16)
    nt = (((1,), (1,)), ((), ()))
    bias_col = bcol_ref[...]
    bias_row = brow_ref[...]

    def log_sigmoid(z):
        return jnp.minimum(z, 0.0) - jnp.log(1.0 + jnp.exp(-jnp.abs(z)))

    def chunk(ci, carry):
        r0 = pl.multiple_of(ci * c, c)
        rows = pl.ds(r0, c)
        q_all = qbuf[pl.ds(pad + r0, c), :]
        k_all = kbuf[pl.ds(pad + r0, c), :]
        v_all = v_ref[rows, :]
        gc = gcol_ref[rows, :] + bias_col
        f_hi, f_mid, f_lo = _split3(log_sigmoid(gc))
        cum_cols = (jnp.dot(tril, f_hi, preferred_element_type=F32) + jnp.dot(tril, f_mid, preferred_element_type=F32)
                    + jnp.dot(tril, f_lo, preferred_element_type=F32))
        gr = grow_ref[ci] + bias_row
        r_hi, r_mid, r_lo = _split3(log_sigmoid(gr))
        cum_rows = (lax.dot_general(r_hi, tril, nt, preferred_element_type=F32)
                    + lax.dot_general(r_mid, tril, nt, preferred_element_type=F32)
                    + lax.dot_general(r_lo, tril, nt, preferred_element_type=F32))
        outs = []
        for hh in range(REC_HEADS):
            cols = slice(hh * REC_DIM, (hh + 1) * REC_DIM)
            q, k, v = q_all[:, cols], k_all[:, cols], v_all[:, cols]
            i_col = gc[:, hh:hh + 1]
            i_row = gr[hh:hh + 1, :]
            cum_col = cum_cols[:, REC_HEADS + hh:REC_HEADS + hh + 1]
            cum_s = cum_rows[REC_HEADS + hh:REC_HEADS + hh + 1, :]
            m_prev = m_ref[hh:hh + 1, 0:1]
            log_d = jnp.where(causal, cum_col - cum_s + i_row, NEG_BIG)
            log_inter = cum_col + m_prev
            m_t = jnp.maximum(jnp.max(log_d, axis=1, keepdims=True), log_inter)
            w_intra = jnp.exp(log_d - m_t)
            w_inter = jnp.exp(log_inter - m_t)
            qb = q.astype(BF16)
            qk = lax.dot_general(qb, k.astype(BF16), nt, preferred_element_type=F32) * w_intra
            c_st = c_ref[hh]
            n_st = n_ref[hh:hh + 1, :]
            num = jnp.dot(qk.astype(BF16), v.astype(BF16), preferred_element_type=F32) \
                + w_inter * jnp.dot(qb, c_st.astype(BF16), preferred_element_type=F32)
            den = jnp.sum(qk, axis=1, keepdims=True) + w_inter * jnp.sum(q * n_st, axis=1, keepdims=True)
            outs.append(num / jnp.maximum(jnp.abs(den), jnp.exp(-m_t)))

            last = cum_col[c - 1:c, :]
            log_w = last - cum_col + i_col
            m_new = jnp.maximum(last + m_prev, jnp.max(log_w, axis=0, keepdims=True))
            w_s = jnp.exp(log_w - m_new)
            decay = jnp.exp(last + m_prev - m_new)
            kw = w_s * k
            c_ref[hh] = decay * c_st + jnp.dot(kw.T.astype(BF16), v.astype(BF16), preferred_element_type=F32)
            n_ref[hh:hh + 1, :] = decay * n_st + jnp.sum(kw, axis=0, keepdims=True)
            m_ref[hh:hh + 1, :] = jnp.broadcast_to(m_new, (1, LANES))
        o_ref[rows, :] = (_sigmoid(og_ref[rows, :]) * jnp.concatenate(outs, axis=1)).astype(o_ref.dtype)
        return carry

    lax.fori_loop(0, sblk // c, chunk, 0)


def _mlstm(proj, gates_t, conv_w, i_bias, f_bias, batch, seq, sblk):
    ns = seq // sblk
    bias = jnp.concatenate([i_bias, f_bias]).astype(F32)
    bias_col = jnp.pad(bias, (0, LANES - bias.shape[0])).reshape(1, LANES)
    bias_row = bias.reshape(2 * REC_HEADS, 1)

    def col(off):
        return lambda b, s: (b * ns + s, off // REC_WIDTH)

    blk = (sblk, REC_WIDTH)
    return pl.pallas_call(
        functools.partial(_mlstm_kernel, sblk=sblk),
        grid=(batch, ns),
        in_specs=[
            pl.BlockSpec((1, LANES), lambda b, s: (0, 0)),
            pl.BlockSpec((2 * REC_HEADS, 1), lambda b, s: (0, 0)),
            pl.BlockSpec(blk, col(OFF_QKM)),
            pl.BlockSpec(blk, col(OFF_QKM + REC_WIDTH)),
            pl.BlockSpec(blk, col(OFF_VM)),
            pl.BlockSpec(blk, col(OFF_OM)),
            pl.BlockSpec((sblk, LANES), lambda b, s: (b * ns + s, OFF_GATE // LANES)),
            pl.BlockSpec((sblk // CHUNK, 2 * REC_HEADS, CHUNK), lambda b, s: (b * ns + s, 0, 0)),
            pl.BlockSpec((MLSTM_CONV, REC_WIDTH), lambda b, s: (0, 0)),
            pl.BlockSpec((MLSTM_CONV, REC_WIDTH), lambda b, s: (0, 1)),
        ],
        out_specs=pl.BlockSpec(blk, lambda b, s: (b * ns + s, 0)),
        out_shape=jax.ShapeDtypeStruct((batch * seq, REC_WIDTH), BF16),
        scratch_shapes=[
            pltpu.VMEM((sblk + 8, REC_WIDTH), F32),
            pltpu.VMEM((sblk + 8, REC_WIDTH), F32),
            pltpu.VMEM((REC_HEADS, REC_DIM, REC_DIM), F32),
            pltpu.VMEM((SUBLANES, REC_DIM), F32),
            pltpu.VMEM((SUBLANES, LANES), F32),
        ],
        compiler_params=_cparams("parallel", "arbitrary"),
        name="mlstm_scan",
    )(bias_col, bias_row, proj, proj, proj, proj, proj, gates_t, conv_w, conv_w)


def _out_proj_kernel(x_ref, ya_ref, yh_ref, ym_ref, wa_ref, wh_ref, wm_ref, o_ref):
    acc = jnp.dot(ya_ref[...], wa_ref[...], preferred_element_type=F32)
    acc = acc + jnp.dot(yh_ref[...], wh_ref[...], preferred_element_type=F32)
    acc = acc + jnp.dot(ym_ref[...], wm_ref[...], preferred_element_type=F32)
    o_ref[...] = x_ref[...] + acc


def _out_proj(x, ya, yh, ym, w_out, *, tm, tn):
    t, d = x.shape
    ka, kh = ya.shape[1], yh.shape[1]
    wa, wh, wm = w_out[:ka], w_out[ka:ka + kh], w_out[ka + kh:]
    return pl.pallas_call(
        _out_proj_kernel,
        grid=(t // tm, d // tn),
        in_specs=[
            pl.BlockSpec((tm, tn), lambda i, j: (i, j)),
            pl.BlockSpec((tm, ka), lambda i, j: (i, 0)),
            pl.BlockSpec((tm, kh), lambda i, j: (i, 0)),
            pl.BlockSpec((tm, ym.shape[1]), lambda i, j: (i, 0)),
            pl.BlockSpec((ka, tn), lambda i, j: (0, j)),
            pl.BlockSpec((kh, tn), lambda i, j: (0, j)),
            pl.BlockSpec((wm.shape[0], tn), lambda i, j: (0, j)),
        ],
        out_specs=pl.BlockSpec((tm, tn), lambda i, j: (i, j)),
        out_shape=jax.ShapeDtypeStruct((t, d), F32),
        compiler_params=_cparams("parallel", "arbitrary"),
        name="out_proj",
    )(x, ya, yh, ym, wa, wh, wm)


def _extract_top(s, payload=None):
    r = s.shape[0]
    rows = lax.broadcasted_iota(jnp.int32, s.shape, 0).astype(F32)
    vals, picks = [], []
    for _ in range(PEER_TOPK):
        m = jnp.max(s, axis=0, keepdims=True)
        i = jnp.min(jnp.where(s == m, rows, float(r)), axis=0, keepdims=True)
        hit = rows == i
        s = jnp.where(hit, -jnp.inf, s)
        vals.append(m)
        picks.append(i if payload is None else jnp.sum(jnp.where(hit, payload, 0.0), axis=0, keepdims=True))
    return jnp.concatenate(vals, axis=0), jnp.concatenate(picks, axis=0)


def _route_kernel(q_ref, sk_ref, idx_ref, gate_ref):
    nt = (((1,), (1,)), ((), ()))
    ids, gates = [], []
    for h in range(PEER_HEADS):
        base = h * 2 * PEER_HALF
        s1 = lax.dot_general(sk_ref[0, h], q_ref[:, base:base + PEER_HALF], nt,
                             precision=HIGHEST, preferred_element_type=F32)
        s2 = lax.dot_general(sk_ref[1, h], q_ref[:, base + PEER_HALF:base + 2 * PEER_HALF], nt,
                             precision=HIGHEST, preferred_element_type=F32)
        v1, i1 = _extract_top(s1)
        v2, i2 = _extract_top(s2)
        sub = lax.broadcasted_iota(jnp.int32, (SUBLANES, v2.shape[1]), 0)
        cands = [v1[0:1] + v2]
        cidxs = [(i1[0:1] * PEER_NKEYS + i2) * PEER_ROWS]
        for a in range(1, PEER_TOPK):
            cands.append(jnp.where(sub < PEER_TOPK // (a + 1), v1[a:a + 1] + v2[0:SUBLANES], -jnp.inf))
            cidxs.append((i1[a:a + 1] * PEER_NKEYS + i2[0:SUBLANES]) * PEER_ROWS)
        top_s, eid = _extract_top(jnp.concatenate(cands, axis=0), jnp.concatenate(cidxs, axis=0))
        p = jnp.exp(top_s - jnp.max(top_s, axis=0, keepdims=True))
        gates.append(p / jnp.sum(p, axis=0, keepdims=True))
        ids.append(eid)
    idx_ref[...] = jnp.concatenate(ids, axis=0).T.astype(jnp.int32)
    gate_ref[...] = jnp.concatenate(gates, axis=0).T


def _route(qv, sub_keys, *, tn):
    t = qv.shape[0]
    return pl.pallas_call(
        _route_kernel,
        grid=(t // tn,),
        in_specs=[
            pl.BlockSpec((tn, qv.shape[1]), lambda i: (i, 0)),
            pl.BlockSpec(sub_keys.shape, lambda i: (0, 0, 0, 0)),
        ],
        out_specs=[pl.BlockSpec((tn, PEER_SLOTS), lambda i: (i, 0))] * 2,
        out_shape=[jax.ShapeDtypeStruct((t, PEER_SLOTS), jnp.int32), jax.ShapeDtypeStruct((t, PEER_SLOTS), F32)],
        compiler_params=_cparams("parallel"),
        name="peer_route",
    )(qv, sub_keys)


PEER_RING = 4
PEER_STRIDE = PEER_ROWS + 1
PEER_COLS = D_MODEL // LANES


def _peer_table(u, v):
    e = u.shape[0]
    ub = lax.bitcast_convert_type(u.astype(BF16), jnp.uint16).astype(jnp.uint32)
    vb = lax.bitcast_convert_type(v.astype(BF16), jnp.uint16).astype(jnp.uint32)
    return (ub | (vb << 16)).reshape(e * PEER_ROWS, LANES)


def _peer_kernel(idx_ref, x_ref, g_ref, gate_ref, tab_ref, o_ref, *scratch, tb):
    bufs, sem = scratch[:PEER_RING], scratch[PEER_RING]
    ahead = PEER_RING - 1
    gnorm = g_ref[...]

    def issue(tok, j):
        for k in range(PEER_SLOTS):
            row0 = pl.multiple_of(idx_ref[tok, k], PEER_ROWS)
            pltpu.make_async_copy(tab_ref.at[pl.ds(row0, PEER_ROWS), :],
                                  bufs[j].at[pl.ds(k * PEER_STRIDE, PEER_ROWS), :], sem.at[j]).start()

    def wait(j):
        n = PEER_SLOTS * PEER_ROWS
        pltpu.make_async_copy(tab_ref.at[pl.ds(0, n), :], bufs[j].at[pl.ds(0, n), :], sem.at[j]).wait()

    def words(j, c):
        return bufs[j][pl.ds(c, PEER_SLOTS, stride=PEER_STRIDE), :]

    def compute(tok, j):
        xr = x_ref[pl.ds(tok, 1), :]
        hr = xr * lax.rsqrt(jnp.mean(xr * xr, axis=-1, keepdims=True) + NORM_EPS) * gnorm
        grow = gate_ref[pl.ds(tok, 1), :]
        gcol = jnp.broadcast_to(grow, (SUBLANES, PEER_SLOTS)).T[:, 0:1]
        acc = None
        for c in range(PEER_COLS):
            u = lax.bitcast_convert_type(words(j, c) << 16, F32)
            term = u * hr[:, c * LANES:(c + 1) * LANES]
            acc = term if acc is None else acc + term
        pre = jnp.sum(acc, axis=1, keepdims=True)
        act = 0.5 * pre * (1.0 + lax.erf(pre * (2.0 ** -0.5))) * gcol
        outs = []
        for c in range(PEER_COLS):
            v = lax.bitcast_convert_type(words(j, c) & jnp.uint32(0xFFFF0000), F32)
            outs.append(jnp.sum(v * act, axis=0, keepdims=True))
        o_ref[pl.ds(tok, 1), :] = xr + jnp.concatenate(outs, axis=1)

    for t0 in range(ahead):
        issue(t0, t0)

    def group(i, carry):
        for j in range(PEER_RING):
            tok = i * PEER_RING + j
            wait(j)
            issue(jnp.minimum(tok + ahead, tb - 1), (j + ahead) % PEER_RING)
            compute(tok, j)
        return carry

    lax.fori_loop(0, tb // PEER_RING, group, 0)
    for j in range(ahead):
        wait(j)


def _peer_experts(x, g, idx, gate, table, *, tb):
    t, d = x.shape
    assert tb % PEER_RING == 0 and t % tb == 0
    return pl.pallas_call(
        functools.partial(_peer_kernel, tb=tb),
        grid=(t // tb,),
        in_specs=[
            pl.BlockSpec((tb, PEER_SLOTS), lambda i: (i, 0), memory_space=pltpu.SMEM),
            pl.BlockSpec((tb, d), lambda i: (i, 0)),
            pl.BlockSpec((1, d), lambda i: (0, 0)),
            pl.BlockSpec((tb, PEER_SLOTS), lambda i: (i, 0)),
            pl.BlockSpec(memory_space=pl.ANY),
        ],
        out_specs=pl.BlockSpec((tb, d), lambda i: (i, 0)),
        out_shape=jax.ShapeDtypeStruct((t, d), F32),
        scratch_shapes=[pltpu.VMEM((PEER_SLOTS * PEER_STRIDE, LANES), jnp.uint32) for _ in range(PEER_RING)]
        + [pltpu.SemaphoreType.DMA((PEER_RING,))],
        compiler_params=_cparams("arbitrary"),
        name="peer_experts",
    )(idx, x, g.reshape(1, d), gate, table)


def _final_norm_kernel(x_ref, g_ref, o_ref):
    x = x_ref[...]
    o_ref[...] = x * lax.rsqrt(jnp.mean(x * x, axis=-1, keepdims=True) + NORM_EPS) * g_ref[...]


def _final_norm(x, g, *, tm):
    t, d = x.shape
    return pl.pallas_call(
        _final_norm_kernel,
        grid=(t // tm,),
        in_specs=[pl.BlockSpec((tm, d), lambda i: (i, 0)), pl.BlockSpec((1, d), lambda i: (0, 0))],
        out_specs=pl.BlockSpec((tm, d), lambda i: (i, 0)),
        out_shape=jax.ShapeDtypeStruct((t, d), F32),
        compiler_params=_cparams("parallel"),
        name="final_norm",
    )(x, g.reshape(1, d))


def _mixer_layer(x, batch, seq, norm_g, w_in, w_out, sinks, lower_bound, hgrn_gain, conv_w, i_bias, f_bias,
                 cos_t, sin_t, *, tm, sblk):
    w_in_p = jnp.concatenate([w_in[:, a:b].astype(BF16) for a, b in _IN_GROUPS]
                             + [jnp.zeros((w_in.shape[0], IN_PAD - IN_DIM), BF16)], axis=1)
    proj = _norm_matmul(x, norm_g, w_in_p, tm=2 * tm, tn=IN_PAD // 4)
    gates_t = proj[:, OFF_GATE:OFF_GATE + 2 * REC_HEADS].reshape(-1, CHUNK, 2 * REC_HEADS).transpose(0, 2, 1)
    ya = _attention(proj, sinks, cos_t, sin_t, batch, seq)
    yh = _hgrn(proj, lower_bound, hgrn_gain, batch, seq, sblk)
    ym = _mlstm(proj, gates_t, conv_w, i_bias, f_bias, batch, seq, sblk)
    return _out_proj(x, ya, yh, ym, w_out.astype(BF16), tm=tm, tn=1024)


def _peer_layer(x, norm_g, w_query, sub_keys, expert_u, expert_v, *, tm, tb):
    qv = _norm_matmul(x, norm_g, w_query.astype(BF16), tm=2 * tm, tn=1024)
    idx, gate = _route(qv, sub_keys, tn=LANES)
    return _peer_experts(x, norm_g, idx, gate, _peer_table(expert_u, expert_v), tb=tb)


def _forward(x, norm_mix, w_in, w_out, attn_sinks, hgrn_lb_logits, hgrn_out_norm, mlstm_conv, mlstm_i_bias,
             mlstm_f_bias, norm_ffn, peer_query, peer_sub_keys, peer_u, peer_v, norm_final, *, tm, sblk, tb):
    batch, seq, d = x.shape
    depth = w_in.shape[0]
    p = jax.nn.softmax(hgrn_lb_logits.astype(F32), axis=0)
    lower_bounds = jnp.cumsum(p, axis=0) - p[0:1]
    cos_t, sin_t = _rope_tables(seq)
    xt = x.reshape(batch * seq, d)
    for l in range(depth):
        xt = _mixer_layer(xt, batch, seq, norm_mix[l], w_in[l], w_out[l], attn_sinks[l], lower_bounds[l],
                          hgrn_out_norm[l], mlstm_conv[l], mlstm_i_bias[l], mlstm_f_bias[l], cos_t, sin_t,
                          tm=tm, sblk=sblk)
        xt = _peer_layer(xt, norm_ffn[l], peer_query[l], peer_sub_keys[l], peer_u[l], peer_v[l], tm=tm, tb=tb)
    return _final_norm(xt, norm_final, tm=tm).reshape(batch, seq, d)


def kernel(x, norm_mix, w_in, w_out, attn_sinks, hgrn_lb_logits, hgrn_out_norm, mlstm_conv, mlstm_i_bias, mlstm_f_bias, norm_ffn, peer_query, peer_sub_keys, peer_u, peer_v, norm_final):
    return _forward(x, norm_mix, w_in, w_out, attn_sinks, hgrn_lb_logits, hgrn_out_norm, mlstm_conv,
                    mlstm_i_bias, mlstm_f_bias, norm_ffn, peer_query, peer_sub_keys, peer_u, peer_v, norm_final,
                    tm=512, sblk=512, tb=256)
```

```python
import functools

import jax
import jax.numpy as jnp
import numpy as np
from jax import lax
from jax.experimental import pallas as pl
from jax.experimental.pallas import tpu as pltpu

F32 = jnp.float32
BF16 = jnp.bfloat16
HIGHEST = lax.Precision.HIGHEST

D_MODEL = 2048
ATTN_HEAD_DIM = 64
ATTN_Q_HEADS = 16
ATTN_KV_HEADS = 2
ATTN_GROUP = ATTN_Q_HEADS // ATTN_KV_HEADS
WINDOW = 128
ROPE_THETA = 10000.0
REC_HEADS = 4
REC_DIM = 128
REC_WIDTH = REC_HEADS * REC_DIM
MLSTM_CONV = 4
CHUNK = 64
PEER_HEADS = 8
PEER_NKEYS = 128
PEER_TOPK = 16
PEER_HALF = 128
PEER_SLOTS = PEER_HEADS * PEER_TOPK
NORM_EPS = 1e-6
NEG_BIG = -1e30
TINY = 1e-30

LANES = 128
SUBLANES = 8
PEER_ROWS = 2 * D_MODEL // LANES
VMEM_LIMIT = 48 * 1024 * 1024

OFF_QA = 0
OFF_QH, OFF_FH, OFF_IH, OFF_GH = 1024, 1536, 2048, 2560
OFF_QKM, OFF_VM, OFF_OM = 3072, 4096, 4608
OFF_KA, OFF_VA, OFF_GATE = 5120, 5248, 5376
IN_DIM = 5384
IN_PAD = 5632
_IN_GROUPS = ((0, 1024), (1280, 1792), (1792, 2304), (2304, 2816), (2816, 3328), (3328, 4352), (4352, 4864),
              (4864, 5376), (1024, 1152), (1152, 1280), (5376, 5384))


def _cparams(*sem):
    return pltpu.CompilerParams(dimension_semantics=sem, vmem_limit_bytes=VMEM_LIMIT)


def _sigmoid(x):
    return 1.0 / (1.0 + jnp.exp(-x))


def _silu(x):
    return x * _sigmoid(x)


def _norm_matmul_kernel(x_ref, g_ref, w_ref, o_ref, h_ref):
    @pl.when(pl.program_id(1) == 0)
    def _():
        x = x_ref[...]
        y = x * lax.rsqrt(jnp.mean(x * x, axis=-1, keepdims=True) + NORM_EPS) * g_ref[...]
        h_ref[...] = y.astype(BF16)

    o_ref[...] = jnp.dot(h_ref[...], w_ref[...], preferred_element_type=F32)


def _norm_matmul(x, g, w, *, tm, tn):
    t, d = x.shape
    n = w.shape[1]
    return pl.pallas_call(
        _norm_matmul_kernel,
        grid=(t // tm, n // tn),
        in_specs=[
            pl.BlockSpec((tm, d), lambda i, j: (i, 0)),
            pl.BlockSpec((1, d), lambda i, j: (0, 0)),
            pl.BlockSpec((d, tn), lambda i, j: (0, j)),
        ],
        out_specs=pl.BlockSpec((tm, tn), lambda i, j: (i, j)),
        out_shape=jax.ShapeDtypeStruct((t, n), F32),
        scratch_shapes=[pltpu.VMEM((tm, d), BF16)],
        compiler_params=_cparams("parallel", "arbitrary"),
        name="norm_matmul",
    )(x, g.reshape(1, d), w)


def _attn_kernel(sink_ref, q_ref, kc_ref, kp_ref, vc_ref, vp_ref, cc_ref, sc_ref, cp_ref, sp_ref, o_ref):
    n = pl.program_id(1)
    w = WINDOW
    lane = lax.broadcasted_iota(jnp.int32, (w, LANES), 1)
    first_half = (lane % ATTN_HEAD_DIM) < (ATTN_HEAD_DIM // 2)

    def rope(x, cos, sin_signed):
        rot = jnp.where(first_half, pltpu.roll(x, LANES - 32, 1), pltpu.roll(x, 32, 1))
        return x * cos + rot * sin_signed

    cos_c, sin_c = cc_ref[...], sc_ref[...]
    k_band = jnp.concatenate(
        [rope(kp_ref[...], cp_ref[...], sp_ref[...]), rope(kc_ref[...], cos_c, sin_c)], axis=0)
    v_band = jnp.concatenate([vp_ref[...], vc_ref[...]], axis=0)
    lane2 = lax.broadcasted_iota(jnp.int32, (2 * w, LANES), 1)
    low2 = lane2 < ATTN_HEAD_DIM

    def dup(x):
        sw = pltpu.roll(x, ATTN_HEAD_DIM, 1)
        return jnp.where(low2, x, sw).astype(BF16), jnp.where(low2, sw, x).astype(BF16)

    k_dup, v_dup = dup(k_band), dup(v_band)

    qi = lax.broadcasted_iota(jnp.int32, (w, 2 * w), 0)
    kj = lax.broadcasted_iota(jnp.int32, (w, 2 * w), 1)
    rel = qi + w - kj
    valid = (rel >= 0) & (rel < WINDOW) & ((kj >= w) | (n > 0))
    low = lane < ATTN_HEAD_DIM

    for c in range(ATTN_Q_HEADS // 2):
        qc = rope(q_ref[:, c * LANES:(c + 1) * LANES], cos_c, sin_c)
        kvh = (2 * c) // ATTN_GROUP
        outs = []
        for j in range(2):
            head = 2 * c + j
            qm = jnp.where(low if j == 0 else ~low, qc, 0.0).astype(BF16)
            logits = lax.dot_general(qm, k_dup[kvh], (((1,), (1,)), ((), ())), preferred_element_type=F32)
            logits = jnp.where(valid, logits * (ATTN_HEAD_DIM ** -0.5), NEG_BIG)
            sink = sink_ref[head]
            m = jnp.maximum(jnp.max(logits, axis=-1, keepdims=True), sink)
            p = jnp.exp(logits - m)
            denom = jnp.sum(p, axis=-1, keepdims=True) + jnp.exp(sink - m)
            o = jnp.dot(p.astype(BF16), v_dup[kvh], preferred_element_type=F32)
            outs.append(o / denom)
        o_ref[:, c * LANES:(c + 1) * LANES] = jnp.where(low, outs[0], outs[1]).astype(o_ref.dtype)


def _attention(proj, sinks, cos_t, sin_t, batch, seq):
    nblk = seq // WINDOW
    kcol, vcol = OFF_KA // LANES, OFF_VA // LANES

    def cur(col):
        return lambda b, n: (b * nblk + n, col)

    def prev(col):
        return lambda b, n: (b * nblk + jnp.maximum(n - 1, 0), col)

    blk = (WINDOW, LANES)
    return pl.pallas_call(
        _attn_kernel,
        grid=(batch, nblk),
        in_specs=[
            pl.BlockSpec(memory_space=pltpu.SMEM),
            pl.BlockSpec((WINDOW, ATTN_Q_HEADS * ATTN_HEAD_DIM), cur(0)),
            pl.BlockSpec(blk, cur(kcol)),
            pl.BlockSpec(blk, prev(kcol)),
            pl.BlockSpec(blk, cur(vcol)),
            pl.BlockSpec(blk, prev(vcol)),
            pl.BlockSpec(blk, lambda b, n: (n, 0)),
            pl.BlockSpec(blk, lambda b, n: (n, 0)),
            pl.BlockSpec(blk, lambda b, n: (jnp.maximum(n - 1, 0), 0)),
            pl.BlockSpec(blk, lambda b, n: (jnp.maximum(n - 1, 0), 0)),
        ],
        out_specs=pl.BlockSpec((WINDOW, ATTN_Q_HEADS * ATTN_HEAD_DIM), cur(0)),
        out_shape=jax.ShapeDtypeStruct((batch * seq, ATTN_Q_HEADS * ATTN_HEAD_DIM), BF16),
        compiler_params=_cparams("parallel", "arbitrary"),
        name="swa_attention",
    )(sinks, proj, proj, proj, proj, proj, cos_t, sin_t, cos_t, sin_t)


def _rope_tables(seq):
    half = ATTN_HEAD_DIM // 2
    inv = ROPE_THETA ** (-jnp.arange(half, dtype=F32) / half)
    ang = jnp.arange(seq, dtype=F32)[:, None] * inv[None, :]
    cos, sin = jnp.cos(ang), jnp.sin(ang)
    return jnp.tile(cos, (1, 4)), jnp.concatenate([-sin, sin, -sin, sin], axis=1)


def _hgrn_constants():
    c = CHUNK
    levels = int(np.log2(c))
    t = np.arange(c)[:, None]
    r = np.arange(c)[None, :]
    dq, dk, masks = [], [], [np.eye(c)]
    for lv in range(1, levels + 1):
        half = 1 << (lv - 1)
        upper = (t & half) != 0
        mid_q = t & ~(half - 1)
        dq.append((upper & (r >= mid_q) & (r <= t)).astype(np.float32))
        mid_k = (t & ~(half - 1)) + half
        dk.append((~upper & (r >= t + 1) & (r <= mid_k - 1)).astype(np.float32))
        same = (t >> lv) == (r >> lv)
        masks.append((same & upper & ((r & half) == 0)).astype(np.float32))
    tril = (r <= t).astype(np.float32)
    rem = (r > t).astype(np.float32)
    dall = np.concatenate(dq + dk + [tril, rem], axis=0)
    return jnp.asarray(dall), jnp.asarray(np.concatenate(masks, axis=0))


def _hgrn_kernel(q_ref, f_ref, i_ref, g_ref, lb_ref, gain_ref, dall_ref, mask_ref, o_ref, st_ref, *, sblk):
    c = CHUNK
    levels = int(np.log2(c))

    @pl.when(pl.program_id(1) == 0)
    def _():
        st_ref[...] = jnp.zeros_like(st_ref)

    lb = lb_ref[...]
    gain = gain_ref[...]
    dall = dall_ref[...]
    nt = (((1,), (1,)), ((), ()))

    def chunk(ci, carry):
        rows = pl.ds(pl.multiple_of(ci * c, c), c)
        fz = f_ref[rows, :]
        f_gate = lb + (1.0 - lb) * _sigmoid(fz)
        a = jnp.log(jnp.maximum(f_gate, TINY))
        k_all = (1.0 - lb) * _sigmoid(-fz)
        q_all = _silu(q_ref[rows, :])
        v_all = i_ref[rows, :]
        a_hi, a_mid, a_lo = _split3(a)
        e_all = jnp.exp(jnp.dot(dall, a_hi, preferred_element_type=F32)
                        + jnp.dot(dall, a_mid, preferred_element_type=F32)
                        + jnp.dot(dall, a_lo, preferred_element_type=F32))
        base = 2 * levels * c
        outs = []
        for hh in range(REC_HEADS):
            cols = slice(hh * REC_DIM, (hh + 1) * REC_DIM)
            q, k, v, e = q_all[:, cols], k_all[:, cols], v_all[:, cols], e_all[:, cols]
            e_cum = e[base:base + c]
            e_rem = e[base + c:base + 2 * c]
            scores = mask_ref[0:c, :] * lax.dot_general(q.astype(BF16), k.astype(BF16), nt,
                                                        preferred_element_type=F32)
            for lv in range(levels):
                qd = (q * e[lv * c:(lv + 1) * c]).astype(BF16)
                kd = (k * e[(levels + lv) * c:(levels + lv + 1) * c]).astype(BF16)
                scores = scores + mask_ref[(lv + 1) * c:(lv + 2) * c, :] * lax.dot_general(
                    qd, kd, nt, preferred_element_type=F32)

            st = st_ref[hh]
            o = jnp.dot(scores.astype(BF16), v.astype(BF16), preferred_element_type=F32)
            o = o + lax.dot_general((q * e_cum).astype(BF16), st.astype(BF16), nt, preferred_element_type=F32)
            k_dec = (k * e_rem).astype(BF16)
            st_ref[hh] = st * e_cum[c - 1:c, :] + jnp.dot(v.T.astype(BF16), k_dec, preferred_element_type=F32)
            outs.append(o * lax.rsqrt(jnp.mean(o * o, axis=-1, keepdims=True) + NORM_EPS))
        y = jnp.concatenate(outs, axis=1) * gain
        o_ref[rows, :] = (y * _silu(g_ref[rows, :])).astype(o_ref.dtype)
        return carry

    lax.fori_loop(0, sblk // c, chunk, 0)


def _hgrn(proj, lower_bound, gain, batch, seq, sblk):
    ns = seq // sblk
    dall, masks = _hgrn_constants()

    def col(off):
        return lambda b, s: (b * ns + s, off // REC_WIDTH)

    blk = (sblk, REC_WIDTH)
    vec = pl.BlockSpec((1, REC_WIDTH), lambda b, s: (0, 0))
    return pl.pallas_call(
        functools.partial(_hgrn_kernel, sblk=sblk),
        grid=(batch, ns),
        in_specs=[
            pl.BlockSpec(blk, col(OFF_QH)),
            pl.BlockSpec(blk, col(OFF_FH)),
            pl.BlockSpec(blk, col(OFF_IH)),
            pl.BlockSpec(blk, col(OFF_GH)),
            vec,
            vec,
            pl.BlockSpec(dall.shape, lambda b, s: (0, 0)),
            pl.BlockSpec(masks.shape, lambda b, s: (0, 0)),
        ],
        out_specs=pl.BlockSpec(blk, lambda b, s: (b * ns + s, 0)),
        out_shape=jax.ShapeDtypeStruct((batch * seq, REC_WIDTH), BF16),
        scratch_shapes=[pltpu.VMEM((REC_HEADS, REC_DIM, REC_DIM), F32)],
        compiler_params=_cparams("parallel", "arbitrary"),
        name="hgrn2_scan",
    )(proj, proj, proj, proj, lower_bound.reshape(1, REC_WIDTH), gain.reshape(1, REC_WIDTH),
      dall.astype(BF16), masks)


def _split3(x):
    hi = x.astype(BF16)
    r = x - hi.astype(F32)
    mid = r.astype(BF16)
    return hi, mid, (r - mid.astype(F32)).astype(BF16)


def _mlstm_kernel(bcol_ref, brow_ref, qx_ref, kx_ref, v_ref, og_ref, gcol_ref, grow_ref, wq_ref, wk_ref, o_ref,
                  qbuf, kbuf, c_ref, n_ref, m_ref, *, sblk):
    c = CHUNK
    pad = 8

    @pl.when(pl.program_id(1) == 0)
    def _():
        qbuf[0:pad, :] = jnp.zeros((pad, REC_WIDTH), F32)
        kbuf[0:pad, :] = jnp.zeros((pad, REC_WIDTH), F32)
        c_ref[...] = jnp.zeros_like(c_ref)
        n_ref[...] = jnp.zeros_like(n_ref)
        m_ref[...] = jnp.zeros_like(m_ref)

    def conv_silu(x_ref, buf, w_ref):
        buf[pad:pad + sblk, :] = x_ref[...]
        acc = jnp.zeros((sblk, REC_WIDTH), F32)
        for j in range(MLSTM_CONV):
            acc = acc + buf[pad - (MLSTM_CONV - 1) + j:pad - (MLSTM_CONV - 1) + j + sblk, :] * w_ref[j:j + 1, :]
        tail = buf[sblk:sblk + pad, :]
        buf[0:pad, :] = tail
        return _silu(acc)

    qbuf[pad:pad + sblk, :] = conv_silu(qx_ref, qbuf, wq_ref)
    kbuf[pad:pad + sblk, :] = conv_silu(kx_ref, kbuf, wk_ref) * (REC_DIM ** -0.5)

    ti = lax.broadcasted_iota(jnp.int32, (c, c), 0)
    si = lax.broadcasted_iota(jnp.int32, (c, c), 1)
    causal = si <= ti
    tril = causal.astype(BF16)
    nt = (((1,), (1,)), ((), ()))
    bias_col = bcol_ref[...]
    bias_row = brow_ref[...]

    def log_sigmoid(z):
        return jnp.minimum(z, 0.0) - jnp.log(1.0 + jnp.exp(-jnp.abs(z)))

    def chunk(ci, carry):
        r0 = pl.multiple_of(ci * c, c)
        rows = pl.ds(r0, c)
        q_all = qbuf[pl.ds(pad + r0, c), :]
        k_all = kbuf[pl.ds(pad + r0, c), :]
        v_all = v_ref[rows, :]
        gc = gcol_ref[rows, :] + bias_col
        f_hi, f_mid, f_lo = _split3(log_sigmoid(gc))
        cum_cols = (jnp.dot(tril, f_hi, preferred_element_type=F32) + jnp.dot(tril, f_mid, preferred_element_type=F32)
                    + jnp.dot(tril, f_lo, preferred_element_type=F32))
        gr = grow_ref[ci] + bias_row
        r_hi, r_mid, r_lo = _split3(log_sigmoid(gr))
        cum_rows = (lax.dot_general(r_hi, tril, nt, preferred_element_type=F32)
                    + lax.dot_general(r_mid, tril, nt, preferred_element_type=F32)
                    + lax.dot_general(r_lo, tril, nt, preferred_element_type=F32))
        outs = []
        for hh in range(REC_HEADS):
            cols = slice(hh * REC_DIM, (hh + 1) * REC_DIM)
            q, k, v = q_all[:, cols], k_all[:, cols], v_all[:, cols]
            i_col = gc[:, hh:hh + 1]
            i_row = gr[hh:hh + 1, :]
            cum_col = cum_cols[:, REC_HEADS + hh:REC_HEADS + hh + 1]
            cum_s = cum_rows[REC_HEADS + hh:REC_HEADS + hh + 1, :]
            m_prev = m_ref[hh:hh + 1, 0:1]
            log_d = jnp.where(causal, cum_col - cum_s + i_row, NEG_BIG)
            log_inter = cum_col + m_prev
            m_t = jnp.maximum(jnp.max(log_d, axis=1, keepdims=True), log_inter)
            w_intra = jnp.exp(log_d - m_t)
            w_inter = jnp.exp(log_inter - m_t)
            qb = q.astype(BF16)
            qk = lax.dot_general(qb, k.astype(BF16), nt, preferred_element_type=F32) * w_intra
            c_st = c_ref[hh]
            n_st = n_ref[hh:hh + 1, :]
            num = jnp.dot(qk.astype(BF16), v.astype(BF16), preferred_element_type=F32) \
                + w_inter * jnp.dot(qb, c_st.astype(BF16), preferred_element_type=F32)
            den = jnp.sum(qk, axis=1, keepdims=True) + w_inter * jnp.sum(q * n_st, axis=1, keepdims=True)
            outs.append(num / jnp.maximum(jnp.abs(den), jnp.exp(-m_t)))

            last = cum_col[c - 1:c, :]
            log_w = last - cum_col + i_col
            m_new = jnp.maximum(last + m_prev, jnp.max(log_w, axis=0, keepdims=True))
            w_s = jnp.exp(log_w - m_new)
            decay = jnp.exp(last + m_prev - m_new)
            kw = w_s * k
            c_ref[hh] = decay * c_st + jnp.dot(kw.T.astype(BF16), v.astype(BF16), preferred_element_type=F32)
            n_ref[hh:hh + 1, :] = decay * n_st + jnp.sum(kw, axis=0, keepdims=True)
            m_ref[hh:hh + 1, :] = jnp.broadcast_to(m_new, (1, LANES))
        o_ref[rows, :] = (_sigmoid(og_ref[rows, :]) * jnp.concatenate(outs, axis=1)).astype(o_ref.dtype)
        return carry

    lax.fori_loop(0, sblk // c, chunk, 0)


def _mlstm(proj, gates_t, conv_w, i_bias, f_bias, batch, seq, sblk):
    ns = seq // sblk
    bias = jnp.concatenate([i_bias, f_bias]).astype(F32)
    bias_col = jnp.pad(bias, (0, LANES - bias.shape[0])).reshape(1, LANES)
    bias_row = bias.reshape(2 * REC_HEADS, 1)

    def col(off):
        return lambda b, s: (b * ns + s, off // REC_WIDTH)

    blk = (sblk, REC_WIDTH)
    return pl.pallas_call(
        functools.partial(_mlstm_kernel, sblk=sblk),
        grid=(batch, ns),
        in_specs=[
            pl.BlockSpec((1, LANES), lambda b, s: (0, 0)),
            pl.BlockSpec((2 * REC_HEADS, 1), lambda b, s: (0, 0)),
            pl.BlockSpec(blk, col(OFF_QKM)),
            pl.BlockSpec(blk, col(OFF_QKM + REC_WIDTH)),
            pl.BlockSpec(blk, col(OFF_VM)),
            pl.BlockSpec(blk, col(OFF_OM)),
            pl.BlockSpec((sblk, LANES), lambda b, s: (b * ns + s, OFF_GATE // LANES)),
            pl.BlockSpec((sblk // CHUNK, 2 * REC_HEADS, CHUNK), lambda b, s: (b * ns + s, 0, 0)),
            pl.BlockSpec((MLSTM_CONV, REC_WIDTH), lambda b, s: (0, 0)),
            pl.BlockSpec((MLSTM_CONV, REC_WIDTH), lambda b, s: (0, 1)),
        ],
        out_specs=pl.BlockSpec(blk, lambda b, s: (b * ns + s, 0)),
        out_shape=jax.ShapeDtypeStruct((batch * seq, REC_WIDTH), BF16),
        scratch_shapes=[
            pltpu.VMEM((sblk + 8, REC_WIDTH), F32),
            pltpu.VMEM((sblk + 8, REC_WIDTH), F32),
            pltpu.VMEM((REC_HEADS, REC_DIM, REC_DIM), F32),
            pltpu.VMEM((SUBLANES, REC_DIM), F32),
            pltpu.VMEM((SUBLANES, LANES), F32),
        ],
        compiler_params=_cparams("parallel", "arbitrary"),
        name="mlstm_scan",
    )(bias_col, bias_row, proj, proj, proj, proj, proj, gates_t, conv_w, conv_w)


def _out_proj_kernel(x_ref, ya_ref, yh_ref, ym_ref, wa_ref, wh_ref, wm_ref, o_ref):
    acc = jnp.dot(ya_ref[...], wa_ref[...], preferred_element_type=F32)
    acc = acc + jnp.dot(yh_ref[...], wh_ref[...], preferred_element_type=F32)
    acc = acc + jnp.dot(ym_ref[...], wm_ref[...], preferred_element_type=F32)
    o_ref[...] = x_ref[...] + acc


def _out_proj(x, ya, yh, ym, w_out, *, tm, tn):
    t, d = x.shape
    ka, kh = ya.shape[1], yh.shape[1]
    wa, wh, wm = w_out[:ka], w_out[ka:ka + kh], w_out[ka + kh:]
    return pl.pallas_call(
        _out_proj_kernel,
        grid=(t // tm, d // tn),
        in_specs=[
            pl.BlockSpec((tm, tn), lambda i, j: (i, j)),
            pl.BlockSpec((tm, ka), lambda i, j: (i, 0)),
            pl.BlockSpec((tm, kh), lambda i, j: (i, 0)),
            pl.BlockSpec((tm, ym.shape[1]), lambda i, j: (i, 0)),
            pl.BlockSpec((ka, tn), lambda i, j: (0, j)),
            pl.BlockSpec((kh, tn), lambda i, j: (0, j)),
            pl.BlockSpec((wm.shape[0], tn), lambda i, j: (0, j)),
        ],
        out_specs=pl.BlockSpec((tm, tn), lambda i, j: (i, j)),
        out_shape=jax.ShapeDtypeStruct((t, d), F32),
        compiler_params=_cparams("parallel", "arbitrary"),
        name="out_proj",
    )(x, ya, yh, ym, wa, wh, wm)


def _extract_top_halved(s):
    half = s.shape[0] // 2
    rows = lax.broadcasted_iota(jnp.int32, (half, s.shape[1]), 0).astype(F32)
    lo, hi = s[:half], s[half:]
    lo_wins = lo >= hi
    top = jnp.where(lo_wins, lo, hi)
    back = jnp.where(lo_wins, hi, lo)
    top_i = jnp.where(lo_wins, rows, rows + half)
    back_i = jnp.where(lo_wins, rows + half, rows)
    vals, picks = [], []
    for _ in range(PEER_TOPK):
        m = jnp.max(top, axis=0, keepdims=True)
        i = jnp.min(jnp.where(top == m, top_i, float(2 * half)), axis=0, keepdims=True)
        hit = top_i == i
        vals.append(m)
        picks.append(i)
        top = jnp.where(hit, back, top)
        top_i = jnp.where(hit, back_i, top_i)
        back = jnp.where(hit, -jnp.inf, back)
    return jnp.concatenate(vals, axis=0), jnp.concatenate(picks, axis=0)


def _top_pair_sums(v1, i1, v2, i2):
    k = v1.shape[0]
    arow = lax.broadcasted_iota(jnp.int32, v1.shape, 0).astype(F32)
    ptr = jnp.zeros_like(v1)
    v2_front = jnp.broadcast_to(v2[0:1], v1.shape)
    vals, rows = [], []
    for _ in range(PEER_TOPK):
        front = v1 + v2_front
        m = jnp.max(front, axis=0, keepdims=True)
        a_star = jnp.min(jnp.where(front == m, arow, float(k)), axis=0, keepdims=True)
        hit = arow == a_star
        b_star = jnp.sum(jnp.where(hit, ptr, 0.0), axis=0, keepdims=True)
        i1_star = jnp.sum(jnp.where(hit, i1, 0.0), axis=0, keepdims=True)
        i2_star = jnp.sum(jnp.where(arow == b_star, i2, 0.0), axis=0, keepdims=True)
        vals.append(m)
        rows.append((i1_star * PEER_NKEYS + i2_star) * PEER_ROWS)
        b_next = b_star + 1.0
        v2_next = jnp.sum(jnp.where(arow == b_next, v2, 0.0), axis=0, keepdims=True)
        v2_next = jnp.where(b_next >= k, -jnp.inf, v2_next)
        v2_front = jnp.where(hit, v2_next, v2_front)
        ptr = jnp.where(hit, b_next, ptr)
    return jnp.concatenate(vals, axis=0), jnp.concatenate(rows, axis=0)


def _route_kernel(q_ref, sk_ref, idx_ref, gate_ref):
    nt = (((1,), (1,)), ((), ()))
    ids, gates = [], []
    for h in range(PEER_HEADS):
        base = h * 2 * PEER_HALF
        s1 = lax.dot_general(sk_ref[0, h], q_ref[:, base:base + PEER_HALF], nt,
                             precision=HIGHEST, preferred_element_type=F32)
        s2 = lax.dot_general(sk_ref[1, h], q_ref[:, base + PEER_HALF:base + 2 * PEER_HALF], nt,
                             precision=HIGHEST, preferred_element_type=F32)
        v1, i1 = _extract_top_halved(s1)
        v2, i2 = _extract_top_halved(s2)
        top_s, eid = _top_pair_sums(v1, i1, v2, i2)
        p = jnp.exp(top_s - jnp.max(top_s, axis=0, keepdims=True))
        gates.append(p / jnp.sum(p, axis=0, keepdims=True))
        ids.append(eid)
    idx_ref[...] = jnp.concatenate(ids, axis=0).T.astype(jnp.int32)
    gate_ref[...] = jnp.concatenate(gates, axis=0).T


def _route(qv, sub_keys, *, tn):
    t = qv.shape[0]
    return pl.pallas_call(
        _route_kernel,
        grid=(t // tn,),
        in_specs=[
            pl.BlockSpec((tn, qv.shape[1]), lambda i: (i, 0)),
            pl.BlockSpec(sub_keys.shape, lambda i: (0, 0, 0, 0)),
        ],
        out_specs=[pl.BlockSpec((tn, PEER_SLOTS), lambda i: (i, 0))] * 2,
        out_shape=[jax.ShapeDtypeStruct((t, PEER_SLOTS), jnp.int32), jax.ShapeDtypeStruct((t, PEER_SLOTS), F32)],
        compiler_params=_cparams("parallel"),
        name="peer_route",
    )(qv, sub_keys)


PEER_RING = 8
PEER_STRIDE = PEER_ROWS + 1
PEER_COLS = D_MODEL // LANES


def _peer_table(u, v):
    e = u.shape[0]
    return jnp.concatenate([u.reshape(e, PEER_COLS, LANES), v.reshape(e, PEER_COLS, LANES)],
                           axis=1).reshape(e * PEER_ROWS, LANES)


def _peer_kernel(idx_ref, x_ref, g_ref, gate_ref, tab_ref, o_ref, *scratch, tb):
    bufs, sem = scratch[:PEER_RING], scratch[PEER_RING]
    ahead = PEER_RING - 1
    gnorm = g_ref[...]

    def issue(tok, j):
        for k in range(PEER_SLOTS):
            row0 = pl.multiple_of(idx_ref[tok, k], PEER_ROWS)
            pltpu.make_async_copy(tab_ref.at[pl.ds(row0, PEER_ROWS), :],
                                  bufs[j].at[pl.ds(k * PEER_STRIDE, PEER_ROWS), :], sem.at[j]).start()

    def wait(j):
        n = PEER_SLOTS * PEER_ROWS
        pltpu.make_async_copy(tab_ref.at[pl.ds(0, n), :], bufs[j].at[pl.ds(0, n), :], sem.at[j]).wait()

    def words(j, c):
        return bufs[j][pl.ds(c, PEER_SLOTS, stride=PEER_STRIDE), :]

    def compute(tok, j):
        xr = x_ref[pl.ds(tok, 1), :]
        hr = xr * lax.rsqrt(jnp.mean(xr * xr, axis=-1, keepdims=True) + NORM_EPS) * gnorm
        grow = gate_ref[pl.ds(tok, 1), :]
        gcol = jnp.broadcast_to(grow, (SUBLANES, PEER_SLOTS)).T[:, 0:1]
        acc = None
        for c in range(PEER_COLS):
            term = words(j, c) * hr[:, c * LANES:(c + 1) * LANES]
            acc = term if acc is None else acc + term
        pre = jnp.sum(acc, axis=1, keepdims=True)
        act = 0.5 * pre * (1.0 + lax.erf(pre * (2.0 ** -0.5))) * gcol
        outs = []
        for c in range(PEER_COLS):
            outs.append(jnp.sum(words(j, PEER_COLS + c) * act, axis=0, keepdims=True))
        o_ref[pl.ds(tok, 1), :] = xr + jnp.concatenate(outs, axis=1)

    for t0 in range(ahead):
        issue(t0, t0)

    def group(i, carry):
        for j in range(PEER_RING):
            tok = i * PEER_RING + j
            wait(j)
            issue(jnp.minimum(tok + ahead, tb - 1), (j + ahead) % PEER_RING)
            compute(tok, j)
        return carry

    lax.fori_loop(0, tb // PEER_RING, group, 0)
    for j in range(ahead):
        wait(j)


def _peer_experts(x, g, idx, gate, table, *, tb):
    t, d = x.shape
    assert tb % PEER_RING == 0 and t % tb == 0
    return pl.pallas_call(
        functools.partial(_peer_kernel, tb=tb),
        grid=(t // tb,),
        in_specs=[
            pl.BlockSpec((tb, PEER_SLOTS), lambda i: (i, 0), memory_space=pltpu.SMEM),
            pl.BlockSpec((tb, d), lambda i: (i, 0)),
            pl.BlockSpec((1, d), lambda i: (0, 0)),
            pl.BlockSpec((tb, PEER_SLOTS), lambda i: (i, 0)),
            pl.BlockSpec(memory_space=pl.ANY),
        ],
        out_specs=pl.BlockSpec((tb, d), lambda i: (i, 0)),
        out_shape=jax.ShapeDtypeStruct((t, d), F32),
        scratch_shapes=[pltpu.VMEM((PEER_SLOTS * PEER_STRIDE, LANES), F32) for _ in range(PEER_RING)]
        + [pltpu.SemaphoreType.DMA((PEER_RING,))],
        compiler_params=_cparams("arbitrary"),
        name="peer_experts",
    )(idx, x, g.reshape(1, d), gate, table)


def _final_norm_kernel(x_ref, g_ref, o_ref):
    x = x_ref[...]
    o_ref[...] = x * lax.rsqrt(jnp.mean(x * x, axis=-1, keepdims=True) + NORM_EPS) * g_ref[...]


def _final_norm(x, g, *, tm):
    t, d = x.shape
    return pl.pallas_call(
        _final_norm_kernel,
        grid=(t // tm,),
        in_specs=[pl.BlockSpec((tm, d), lambda i: (i, 0)), pl.BlockSpec((1, d), lambda i: (0, 0))],
        out_specs=pl.BlockSpec((tm, d), lambda i: (i, 0)),
        out_shape=jax.ShapeDtypeStruct((t, d), F32),
        compiler_params=_cparams("parallel"),
        name="final_norm",
    )(x, g.reshape(1, d))


def _mixer_layer(x, batch, seq, norm_g, w_in, w_out, sinks, lower_bound, hgrn_gain, conv_w, i_bias, f_bias,
                 cos_t, sin_t, *, tm, sblk):
    w_in_p = jnp.concatenate([w_in[:, a:b].astype(BF16) for a, b in _IN_GROUPS]
                             + [jnp.zeros((w_in.shape[0], IN_PAD - IN_DIM), BF16)], axis=1)
    proj = _norm_matmul(x, norm_g, w_in_p, tm=2 * tm, tn=IN_PAD // 4)
    gates_t = proj[:, OFF_GATE:OFF_GATE + 2 * REC_HEADS].reshape(-1, CHUNK, 2 * REC_HEADS).transpose(0, 2, 1)
    ya = _attention(proj, sinks, cos_t, sin_t, batch, seq)
    yh = _hgrn(proj, lower_bound, hgrn_gain, batch, seq, sblk)
    ym = _mlstm(proj, gates_t, conv_w, i_bias, f_bias, batch, seq, sblk)
    return _out_proj(x, ya, yh, ym, w_out.astype(BF16), tm=tm, tn=w_out.shape[1])


def _peer_layer(x, norm_g, w_query, sub_keys, expert_u, expert_v, *, tm, tb):
    qv = _norm_matmul(x, norm_g, w_query.astype(BF16), tm=2 * tm, tn=w_query.shape[1])
    idx, gate = _route(qv, sub_keys, tn=LANES)
    return _peer_experts(x, norm_g, idx, gate, _peer_table(expert_u, expert_v), tb=tb)


def _forward(x, norm_mix, w_in, w_out, attn_sinks, hgrn_lb_logits, hgrn_out_norm, mlstm_conv, mlstm_i_bias,
             mlstm_f_bias, norm_ffn, peer_query, peer_sub_keys, peer_u, peer_v, norm_final, *, tm, sblk, tb):
    batch, seq, d = x.shape
    depth = w_in.shape[0]
    p = jax.nn.softmax(hgrn_lb_logits.astype(F32), axis=0)
    lower_bounds = jnp.cumsum(p, axis=0) - p[0:1]
    cos_t, sin_t = _rope_tables(seq)
    xt = x.reshape(batch * seq, d)
    for l in range(depth):
        xt = _mixer_layer(xt, batch, seq, norm_mix[l], w_in[l], w_out[l], attn_sinks[l], lower_bounds[l],
                          hgrn_out_norm[l], mlstm_conv[l], mlstm_i_bias[l], mlstm_f_bias[l], cos_t, sin_t,
                          tm=tm, sblk=sblk)
        xt = _peer_layer(xt, norm_ffn[l], peer_query[l], peer_sub_keys[l], peer_u[l], peer_v[l], tm=tm, tb=tb)
    return _final_norm(xt, norm_final, tm=tm).reshape(batch, seq, d)


def kernel(x, norm_mix, w_in, w_out, attn_sinks, hgrn_lb_logits, hgrn_out_norm, mlstm_conv, mlstm_i_bias, mlstm_f_bias, norm_ffn, peer_query, peer_sub_keys, peer_u, peer_v, norm_final):
    return _forward(x, norm_mix, w_in, w_out, attn_sinks, hgrn_lb_logits, hgrn_out_norm, mlstm_conv,
                    mlstm_i_bias, mlstm_f_bias, norm_ffn, peer_query, peer_sub_keys, peer_u, peer_v, norm_final,
                    tm=512, sblk=512, tb=512)
```

```python
import functools

import jax
import jax.numpy as jnp
import numpy as np
from jax import lax
from jax.experimental import pallas as pl
from jax.experimental.pallas import tpu as pltpu

F32 = jnp.float32
BF16 = jnp.bfloat16
HIGHEST = lax.Precision.HIGHEST

D_MODEL = 2048
ATTN_HEAD_DIM = 64
ATTN_Q_HEADS = 16
ATTN_KV_HEADS = 2
ATTN_GROUP = ATTN_Q_HEADS // ATTN_KV_HEADS
WINDOW = 128
ROPE_THETA = 10000.0
REC_HEADS = 4
REC_DIM = 128
REC_WIDTH = REC_HEADS * REC_DIM
MLSTM_CONV = 4
CHUNK = 64
PEER_HEADS = 8
PEER_NKEYS = 128
PEER_TOPK = 16
PEER_HALF = 128
PEER_SLOTS = PEER_HEADS * PEER_TOPK
NORM_EPS = 1e-6
NEG_BIG = -1e30
TINY = 1e-30

LANES = 128
SUBLANES = 8
PEER_ROWS = D_MODEL // LANES
VMEM_LIMIT = 48 * 1024 * 1024

OFF_QA = 0
OFF_QH, OFF_FH, OFF_IH, OFF_GH = 1024, 1536, 2048, 2560
OFF_QKM, OFF_VM, OFF_OM = 3072, 4096, 4608
OFF_KA, OFF_VA, OFF_GATE = 5120, 5248, 5376
IN_DIM = 5384
IN_PAD = 5632
_IN_GROUPS = ((0, 1024), (1280, 1792), (1792, 2304), (2304, 2816), (2816, 3328), (3328, 4352), (4352, 4864),
              (4864, 5376), (1024, 1152), (1152, 1280), (5376, 5384))


def _cparams(*sem):
    return pltpu.CompilerParams(dimension_semantics=sem, vmem_limit_bytes=VMEM_LIMIT)


def _sigmoid(x):
    return 1.0 / (1.0 + jnp.exp(-x))


def _silu(x):
    return x * _sigmoid(x)


def _norm_matmul_kernel(x_ref, g_ref, w_ref, o_ref, h_ref):
    @pl.when(pl.program_id(1) == 0)
    def _():
        x = x_ref[...]
        y = x * lax.rsqrt(jnp.mean(x * x, axis=-1, keepdims=True) + NORM_EPS) * g_ref[...]
        h_ref[...] = y.astype(BF16)

    o_ref[...] = jnp.dot(h_ref[...], w_ref[...], preferred_element_type=F32)


def _norm_matmul(x, g, w, *, tm, tn):
    t, d = x.shape
    n = w.shape[1]
    return pl.pallas_call(
        _norm_matmul_kernel,
        grid=(t // tm, n // tn),
        in_specs=[
            pl.BlockSpec((tm, d), lambda i, j: (i, 0)),
            pl.BlockSpec((1, d), lambda i, j: (0, 0)),
            pl.BlockSpec((d, tn), lambda i, j: (0, j)),
        ],
        out_specs=pl.BlockSpec((tm, tn), lambda i, j: (i, j)),
        out_shape=jax.ShapeDtypeStruct((t, n), F32),
        scratch_shapes=[pltpu.VMEM((tm, d), BF16)],
        compiler_params=_cparams("parallel", "arbitrary"),
        name="norm_matmul",
    )(x, g.reshape(1, d), w)


def _attn_kernel(sink_ref, q_ref, kc_ref, kp_ref, vc_ref, vp_ref, cc_ref, sc_ref, cp_ref, sp_ref, o_ref):
    n = pl.program_id(1)
    w = WINDOW
    lane = lax.broadcasted_iota(jnp.int32, (w, LANES), 1)
    first_half = (lane % ATTN_HEAD_DIM) < (ATTN_HEAD_DIM // 2)

    def rope(x, cos, sin_signed):
        rot = jnp.where(first_half, pltpu.roll(x, LANES - 32, 1), pltpu.roll(x, 32, 1))
        return x * cos + rot * sin_signed

    cos_c, sin_c = cc_ref[...], sc_ref[...]
    k_band = jnp.concatenate(
        [rope(kp_ref[...], cp_ref[...], sp_ref[...]), rope(kc_ref[...], cos_c, sin_c)], axis=0)
    v_band = jnp.concatenate([vp_ref[...], vc_ref[...]], axis=0)
    lane2 = lax.broadcasted_iota(jnp.int32, (2 * w, LANES), 1)
    low2 = lane2 < ATTN_HEAD_DIM

    def dup(x):
        sw = pltpu.roll(x, ATTN_HEAD_DIM, 1)
        return jnp.where(low2, x, sw).astype(BF16), jnp.where(low2, sw, x).astype(BF16)

    k_dup, v_dup = dup(k_band), dup(v_band)

    qi = lax.broadcasted_iota(jnp.int32, (w, 2 * w), 0)
    kj = lax.broadcasted_iota(jnp.int32, (w, 2 * w), 1)
    rel = qi + w - kj
    valid = (rel >= 0) & (rel < WINDOW) & ((kj >= w) | (n > 0))
    low = lane < ATTN_HEAD_DIM

    for c in range(ATTN_Q_HEADS // 2):
        qc = rope(q_ref[:, c * LANES:(c + 1) * LANES], cos_c, sin_c)
        kvh = (2 * c) // ATTN_GROUP
        outs = []
        for j in range(2):
            head = 2 * c + j
            qm = jnp.where(low if j == 0 else ~low, qc, 0.0).astype(BF16)
            logits = lax.dot_general(qm, k_dup[kvh], (((1,), (1,)), ((), ())), preferred_element_type=F32)
            logits = jnp.where(valid, logits * (ATTN_HEAD_DIM ** -0.5), NEG_BIG)
            sink = sink_ref[head]
            m = jnp.maximum(jnp.max(logits, axis=-1, keepdims=True), sink)
            p = jnp.exp(logits - m)
            denom = jnp.sum(p, axis=-1, keepdims=True) + jnp.exp(sink - m)
            o = jnp.dot(p.astype(BF16), v_dup[kvh], preferred_element_type=F32)
            outs.append(o / denom)
        o_ref[:, c * LANES:(c + 1) * LANES] = jnp.where(low, outs[0], outs[1]).astype(o_ref.dtype)


def _attention(proj, sinks, cos_t, sin_t, batch, seq):
    nblk = seq // WINDOW
    kcol, vcol = OFF_KA // LANES, OFF_VA // LANES

    def cur(col):
        return lambda b, n: (b * nblk + n, col)

    def prev(col):
        return lambda b, n: (b * nblk + jnp.maximum(n - 1, 0), col)

    blk = (WINDOW, LANES)
    return pl.pallas_call(
        _attn_kernel,
        grid=(batch, nblk),
        in_specs=[
            pl.BlockSpec(memory_space=pltpu.SMEM),
            pl.BlockSpec((WINDOW, ATTN_Q_HEADS * ATTN_HEAD_DIM), cur(0)),
            pl.BlockSpec(blk, cur(kcol)),
            pl.BlockSpec(blk, prev(kcol)),
            pl.BlockSpec(blk, cur(vcol)),
            pl.BlockSpec(blk, prev(vcol)),
            pl.BlockSpec(blk, lambda b, n: (n, 0)),
            pl.BlockSpec(blk, lambda b, n: (n, 0)),
            pl.BlockSpec(blk, lambda b, n: (jnp.maximum(n - 1, 0), 0)),
            pl.BlockSpec(blk, lambda b, n: (jnp.maximum(n - 1, 0), 0)),
        ],
        out_specs=pl.BlockSpec((WINDOW, ATTN_Q_HEADS * ATTN_HEAD_DIM), cur(0)),
        out_shape=jax.ShapeDtypeStruct((batch * seq, ATTN_Q_HEADS * ATTN_HEAD_DIM), BF16),
        compiler_params=_cparams("parallel", "arbitrary"),
        name="swa_attention",
    )(sinks, proj, proj, proj, proj, proj, cos_t, sin_t, cos_t, sin_t)


def _rope_tables(seq):
    half = ATTN_HEAD_DIM // 2
    inv = ROPE_THETA ** (-jnp.arange(half, dtype=F32) / half)
    ang = jnp.arange(seq, dtype=F32)[:, None] * inv[None, :]
    cos, sin = jnp.cos(ang), jnp.sin(ang)
    return jnp.tile(cos, (1, 4)), jnp.concatenate([-sin, sin, -sin, sin], axis=1)


def _hgrn_constants():
    c = CHUNK
    levels = int(np.log2(c))
    t = np.arange(c)[:, None]
    r = np.arange(c)[None, :]
    dq, dk, masks = [], [], [np.eye(c)]
    for lv in range(1, levels + 1):
        half = 1 << (lv - 1)
        upper = (t & half) != 0
        mid_q = t & ~(half - 1)
        dq.append((upper & (r >= mid_q) & (r <= t)).astype(np.float32))
        mid_k = (t & ~(half - 1)) + half
        dk.append((~upper & (r >= t + 1) & (r <= mid_k - 1)).astype(np.float32))
        same = (t >> lv) == (r >> lv)
        masks.append((same & upper & ((r & half) == 0)).astype(np.float32))
    tril = (r <= t).astype(np.float32)
    rem = (r > t).astype(np.float32)
    dall = np.concatenate(dq + dk + [tril, rem], axis=0)
    return jnp.asarray(dall), jnp.asarray(np.concatenate(masks, axis=0))


def _hgrn_kernel(q_ref, f_ref, i_ref, g_ref, lb_ref, gain_ref, dall_ref, mask_ref, o_ref, st_ref, *, sblk):
    c = CHUNK
    levels = int(np.log2(c))

    @pl.when(pl.program_id(1) == 0)
    def _():
        st_ref[...] = jnp.zeros_like(st_ref)

    lb = lb_ref[...]
    gain = gain_ref[...]
    dall = dall_ref[...]
    nt = (((1,), (1,)), ((), ()))

    def chunk(ci, carry):
        rows = pl.ds(pl.multiple_of(ci * c, c), c)
        fz = f_ref[rows, :]
        f_gate = lb + (1.0 - lb) * _sigmoid(fz)
        a = jnp.log(jnp.maximum(f_gate, TINY))
        k_all = (1.0 - lb) * _sigmoid(-fz)
        q_all = _silu(q_ref[rows, :])
        v_all = i_ref[rows, :]
        a_hi, a_mid, a_lo = _split3(a)
        e_all = jnp.exp(jnp.dot(dall, a_hi, preferred_element_type=F32)
                        + jnp.dot(dall, a_mid, preferred_element_type=F32)
                        + jnp.dot(dall, a_lo, preferred_element_type=F32))
        base = 2 * levels * c
        outs = []
        for hh in range(REC_HEADS):
            cols = slice(hh * REC_DIM, (hh + 1) * REC_DIM)
            q, k, v, e = q_all[:, cols], k_all[:, cols], v_all[:, cols], e_all[:, cols]
            e_cum = e[base:base + c]
            e_rem = e[base + c:base + 2 * c]
            scores = mask_ref[0:c, :] * lax.dot_general(q.astype(BF16), k.astype(BF16), nt,
                                                        preferred_element_type=F32)
            for lv in range(levels):
                qd = (q * e[lv * c:(lv + 1) * c]).astype(BF16)
                kd = (k * e[(levels + lv) * c:(levels + lv + 1) * c]).astype(BF16)
                scores = scores + mask_ref[(lv + 1) * c:(lv + 2) * c, :] * lax.dot_general(
                    qd, kd, nt, preferred_element_type=F32)

            st = st_ref[hh]
            o = jnp.dot(scores.astype(BF16), v.astype(BF16), preferred_element_type=F32)
            o = o + lax.dot_general((q * e_cum).astype(BF16), st.astype(BF16), nt, preferred_element_type=F32)
            k_dec = (k * e_rem).astype(BF16)
            st_ref[hh] = st * e_cum[c - 1:c, :] + jnp.dot(v.T.astype(BF16), k_dec, preferred_element_type=F32)
            outs.append(o * lax.rsqrt(jnp.mean(o * o, axis=-1, keepdims=True) + NORM_EPS))
        y = jnp.concatenate(outs, axis=1) * gain
        o_ref[rows, :] = (y * _silu(g_ref[rows, :])).astype(o_ref.dtype)
        return carry

    lax.fori_loop(0, sblk // c, chunk, 0)


def _hgrn(proj, lower_bound, gain, batch, seq, sblk):
    ns = seq // sblk
    dall, masks = _hgrn_constants()

    def col(off):
        return lambda b, s: (b * ns + s, off // REC_WIDTH)

    blk = (sblk, REC_WIDTH)
    vec = pl.BlockSpec((1, REC_WIDTH), lambda b, s: (0, 0))
    return pl.pallas_call(
        functools.partial(_hgrn_kernel, sblk=sblk),
        grid=(batch, ns),
        in_specs=[
            pl.BlockSpec(blk, col(OFF_QH)),
            pl.BlockSpec(blk, col(OFF_FH)),
            pl.BlockSpec(blk, col(OFF_IH)),
            pl.BlockSpec(blk, col(OFF_GH)),
            vec,
            vec,
            pl.BlockSpec(dall.shape, lambda b, s: (0, 0)),
            pl.BlockSpec(masks.shape, lambda b, s: (0, 0)),
        ],
        out_specs=pl.BlockSpec(blk, lambda b, s: (b * ns + s, 0)),
        out_shape=jax.ShapeDtypeStruct((batch * seq, REC_WIDTH), BF16),
        scratch_shapes=[pltpu.VMEM((REC_HEADS, REC_DIM, REC_DIM), F32)],
        compiler_params=_cparams("parallel", "arbitrary"),
        name="hgrn2_scan",
    )(proj, proj, proj, proj, lower_bound.reshape(1, REC_WIDTH), gain.reshape(1, REC_WIDTH),
      dall.astype(BF16), masks)


def _split3(x):
    hi = x.astype(BF16)
    r = x - hi.astype(F32)
    mid = r.astype(BF16)
    return hi, mid, (r - mid.astype(F32)).astype(BF16)


def _mlstm_kernel(bcol_ref, brow_ref, qx_ref, kx_ref, v_ref, og_ref, gcol_ref, grow_ref, wq_ref, wk_ref, o_ref,
                  qbuf, kbuf, c_ref, n_ref, m_ref, *, sblk):
    c = CHUNK
    pad = 8

    @pl.when(pl.program_id(1) == 0)
    def _():
        qbuf[0:pad, :] = jnp.zeros((pad, REC_WIDTH), F32)
        kbuf[0:pad, :] = jnp.zeros((pad, REC_WIDTH), F32)
        c_ref[...] = jnp.zeros_like(c_ref)
        n_ref[...] = jnp.zeros_like(n_ref)
        m_ref[...] = jnp.zeros_like(m_ref)

    def conv_silu(x_ref, buf, w_ref):
        buf[pad:pad + sblk, :] = x_ref[...]
        acc = jnp.zeros((sblk, REC_WIDTH), F32)
        for j in range(MLSTM_CONV):
            acc = acc + buf[pad - (MLSTM_CONV - 1) + j:pad - (MLSTM_CONV - 1) + j + sblk, :] * w_ref[j:j + 1, :]
        tail = buf[sblk:sblk + pad, :]
        buf[0:pad, :] = tail
        return _silu(acc)

    qbuf[pad:pad + sblk, :] = conv_silu(qx_ref, qbuf, wq_ref)
    kbuf[pad:pad + sblk, :] = conv_silu(kx_ref, kbuf, wk_ref) * (REC_DIM ** -0.5)

    ti = lax.broadcasted_iota(jnp.int32, (c, c), 0)
    si = lax.broadcasted_iota(jnp.int32, (c, c), 1)
    causal = si <= ti
    tril = causal.astype(BF16)
    nt = (((1,), (1,)), ((), ()))
    bias_col = bcol_ref[...]
    bias_row = brow_ref[...]

    def log_sigmoid(z):
        return jnp.minimum(z, 0.0) - jnp.log(1.0 + jnp.exp(-jnp.abs(z)))

    def chunk(ci, carry):
        r0 = pl.multiple_of(ci * c, c)
        rows = pl.ds(r0, c)
        q_all = qbuf[pl.ds(pad + r0, c), :]
        k_all = kbuf[pl.ds(pad + r0, c), :]
        v_all = v_ref[rows, :]
        gc = gcol_ref[rows, :] + bias_col
        f_hi, f_mid, f_lo = _split3(log_sigmoid(gc))
        cum_cols = (jnp.dot(tril, f_hi, preferred_element_type=F32) + jnp.dot(tril, f_mid, preferred_element_type=F32)
                    + jnp.dot(tril, f_lo, preferred_element_type=F32))
        gr = grow_ref[ci] + bias_row
        r_hi, r_mid, r_lo = _split3(log_sigmoid(gr))
        cum_rows = (lax.dot_general(r_hi, tril, nt, preferred_element_type=F32)
                    + lax.dot_general(r_mid, tril, nt, preferred_element_type=F32)
                    + lax.dot_general(r_lo, tril, nt, preferred_element_type=F32))
        outs = []
        for hh in range(REC_HEADS):
            cols = slice(hh * REC_DIM, (hh + 1) * REC_DIM)
            q, k, v = q_all[:, cols], k_all[:, cols], v_all[:, cols]
            i_col = gc[:, hh:hh + 1]
            i_row = gr[hh:hh + 1, :]
            cum_col = cum_cols[:, REC_HEADS + hh:REC_HEADS + hh + 1]
            cum_s = cum_rows[REC_HEADS + hh:REC_HEADS + hh + 1, :]
            m_prev = m_ref[hh:hh + 1, 0:1]
            log_d = jnp.where(causal, cum_col - cum_s + i_row, NEG_BIG)
            log_inter = cum_col + m_prev
            m_t = jnp.maximum(jnp.max(log_d, axis=1, keepdims=True), log_inter)
            w_intra = jnp.exp(log_d - m_t)
            w_inter = jnp.exp(log_inter - m_t)
            qb = q.astype(BF16)
            qk = lax.dot_general(qb, k.astype(BF16), nt, preferred_element_type=F32) * w_intra
            c_st = c_ref[hh]
            n_st = n_ref[hh:hh + 1, :]
            num = jnp.dot(qk.astype(BF16), v.astype(BF16), preferred_element_type=F32) \
                + w_inter * jnp.dot(qb, c_st.astype(BF16), preferred_element_type=F32)
            den = jnp.sum(qk, axis=1, keepdims=True) + w_inter * jnp.sum(q * n_st, axis=1, keepdims=True)
            outs.append(num / jnp.maximum(jnp.abs(den), jnp.exp(-m_t)))

            last = cum_col[c - 1:c, :]
            log_w = last - cum_col + i_col
            m_new = jnp.maximum(last + m_prev, jnp.max(log_w, axis=0, keepdims=True))
            w_s = jnp.exp(log_w - m_new)
            decay = jnp.exp(last + m_prev - m_new)
            kw = w_s * k
            c_ref[hh] = decay * c_st + jnp.dot(kw.T.astype(BF16), v.astype(BF16), preferred_element_type=F32)
            n_ref[hh:hh + 1, :] = decay * n_st + jnp.sum(kw, axis=0, keepdims=True)
            m_ref[hh:hh + 1, :] = jnp.broadcast_to(m_new, (1, LANES))
        o_ref[rows, :] = (_sigmoid(og_ref[rows, :]) * jnp.concatenate(outs, axis=1)).astype(o_ref.dtype)
        return carry

    lax.fori_loop(0, sblk // c, chunk, 0)


def _mlstm(proj, gates_t, conv_w, i_bias, f_bias, batch, seq, sblk):
    ns = seq // sblk
    bias = jnp.concatenate([i_bias, f_bias]).astype(F32)
    bias_col = jnp.pad(bias, (0, LANES - bias.shape[0])).reshape(1, LANES)
    bias_row = bias.reshape(2 * REC_HEADS, 1)

    def col(off):
        return lambda b, s: (b * ns + s, off // REC_WIDTH)

    blk = (sblk, REC_WIDTH)
    return pl.pallas_call(
        functools.partial(_mlstm_kernel, sblk=sblk),
        grid=(batch, ns),
        in_specs=[
            pl.BlockSpec((1, LANES), lambda b, s: (0, 0)),
            pl.BlockSpec((2 * REC_HEADS, 1), lambda b, s: (0, 0)),
            pl.BlockSpec(blk, col(OFF_QKM)),
            pl.BlockSpec(blk, col(OFF_QKM + REC_WIDTH)),
            pl.BlockSpec(blk, col(OFF_VM)),
            pl.BlockSpec(blk, col(OFF_OM)),
            pl.BlockSpec((sblk, LANES), lambda b, s: (b * ns + s, OFF_GATE // LANES)),
            pl.BlockSpec((sblk // CHUNK, 2 * REC_HEADS, CHUNK), lambda b, s: (b * ns + s, 0, 0)),
            pl.BlockSpec((MLSTM_CONV, REC_WIDTH), lambda b, s: (0, 0)),
            pl.BlockSpec((MLSTM_CONV, REC_WIDTH), lambda b, s: (0, 1)),
        ],
        out_specs=pl.BlockSpec(blk, lambda b, s: (b * ns + s, 0)),
        out_shape=jax.ShapeDtypeStruct((batch * seq, REC_WIDTH), BF16),
        scratch_shapes=[
            pltpu.VMEM((sblk + 8, REC_WIDTH), F32),
            pltpu.VMEM((sblk + 8, REC_WIDTH), F32),
            pltpu.VMEM((REC_HEADS, REC_DIM, REC_DIM), F32),
            pltpu.VMEM((SUBLANES, REC_DIM), F32),
            pltpu.VMEM((SUBLANES, LANES), F32),
        ],
        compiler_params=_cparams("parallel", "arbitrary"),
        name="mlstm_scan",
    )(bias_col, bias_row, proj, proj, proj, proj, proj, gates_t, conv_w, conv_w)


def _out_proj_kernel(x_ref, ya_ref, yh_ref, ym_ref, wa_ref, wh_ref, wm_ref, o_ref):
    acc = jnp.dot(ya_ref[...], wa_ref[...], preferred_element_type=F32)
    acc = acc + jnp.dot(yh_ref[...], wh_ref[...], preferred_element_type=F32)
    acc = acc + jnp.dot(ym_ref[...], wm_ref[...], preferred_element_type=F32)
    o_ref[...] = x_ref[...] + acc


def _out_proj(x, ya, yh, ym, w_out, *, tm, tn):
    t, d = x.shape
    ka, kh = ya.shape[1], yh.shape[1]
    wa, wh, wm = w_out[:ka], w_out[ka:ka + kh], w_out[ka + kh:]
    return pl.pallas_call(
        _out_proj_kernel,
        grid=(t // tm, d // tn),
        in_specs=[
            pl.BlockSpec((tm, tn), lambda i, j: (i, j)),
            pl.BlockSpec((tm, ka), lambda i, j: (i, 0)),
            pl.BlockSpec((tm, kh), lambda i, j: (i, 0)),
            pl.BlockSpec((tm, ym.shape[1]), lambda i, j: (i, 0)),
            pl.BlockSpec((ka, tn), lambda i, j: (0, j)),
            pl.BlockSpec((kh, tn), lambda i, j: (0, j)),
            pl.BlockSpec((wm.shape[0], tn), lambda i, j: (0, j)),
        ],
        out_specs=pl.BlockSpec((tm, tn), lambda i, j: (i, j)),
        out_shape=jax.ShapeDtypeStruct((t, d), F32),
        compiler_params=_cparams("parallel", "arbitrary"),
        name="out_proj",
    )(x, ya, yh, ym, wa, wh, wm)


def _extract_top_halved(s):
    half = s.shape[0] // 2
    rows = lax.broadcasted_iota(jnp.int32, (half, s.shape[1]), 0).astype(F32)
    lo, hi = s[:half], s[half:]
    lo_wins = lo >= hi
    top = jnp.where(lo_wins, lo, hi)
    back = jnp.where(lo_wins, hi, lo)
    top_i = jnp.where(lo_wins, rows, rows + half)
    back_i = jnp.where(lo_wins, rows + half, rows)
    vals, picks = [], []
    for _ in range(PEER_TOPK):
        m = jnp.max(top, axis=0, keepdims=True)
        i = jnp.min(jnp.where(top == m, top_i, float(2 * half)), axis=0, keepdims=True)
        hit = top_i == i
        vals.append(m)
        picks.append(i)
        top = jnp.where(hit, back, top)
        top_i = jnp.where(hit, back_i, top_i)
        back = jnp.where(hit, -jnp.inf, back)
    return jnp.concatenate(vals, axis=0), jnp.concatenate(picks, axis=0)


def _top_pair_sums(v1, i1, v2, i2):
    k = v1.shape[0]
    arow = lax.broadcasted_iota(jnp.int32, v1.shape, 0).astype(F32)
    ptr = jnp.zeros_like(v1)
    v2_front = jnp.broadcast_to(v2[0:1], v1.shape)
    vals, rows = [], []
    for _ in range(PEER_TOPK):
        front = v1 + v2_front
        m = jnp.max(front, axis=0, keepdims=True)
        a_star = jnp.min(jnp.where(front == m, arow, float(k)), axis=0, keepdims=True)
        hit = arow == a_star
        b_star = jnp.sum(jnp.where(hit, ptr, 0.0), axis=0, keepdims=True)
        i1_star = jnp.sum(jnp.where(hit, i1, 0.0), axis=0, keepdims=True)
        i2_star = jnp.sum(jnp.where(arow == b_star, i2, 0.0), axis=0, keepdims=True)
        vals.append(m)
        rows.append((i1_star * PEER_NKEYS + i2_star) * PEER_ROWS)
        b_next = b_star + 1.0
        v2_next = jnp.sum(jnp.where(arow == b_next, v2, 0.0), axis=0, keepdims=True)
        v2_next = jnp.where(b_next >= k, -jnp.inf, v2_next)
        v2_front = jnp.where(hit, v2_next, v2_front)
        ptr = jnp.where(hit, b_next, ptr)
    return jnp.concatenate(vals, axis=0), jnp.concatenate(rows, axis=0)


def _route_kernel(q_ref, sk_ref, idx_ref, gate_ref):
    nt = (((1,), (1,)), ((), ()))
    ids, gates = [], []
    for h in range(PEER_HEADS):
        base = h * 2 * PEER_HALF
        s1 = lax.dot_general(sk_ref[0, h], q_ref[:, base:base + PEER_HALF], nt,
                             precision=HIGHEST, preferred_element_type=F32)
        s2 = lax.dot_general(sk_ref[1, h], q_ref[:, base + PEER_HALF:base + 2 * PEER_HALF], nt,
                             precision=HIGHEST, preferred_element_type=F32)
        v1, i1 = _extract_top_halved(s1)
        v2, i2 = _extract_top_halved(s2)
        top_s, eid = _top_pair_sums(v1, i1, v2, i2)
        p = jnp.exp(top_s - jnp.max(top_s, axis=0, keepdims=True))
        gates.append(p / jnp.sum(p, axis=0, keepdims=True))
        ids.append(eid)
    idx_ref[...] = jnp.concatenate(ids, axis=0).T.astype(jnp.int32)
    gate_ref[...] = jnp.concatenate(gates, axis=0).T


def _route(qv, sub_keys, *, tn):
    t = qv.shape[0]
    return pl.pallas_call(
        _route_kernel,
        grid=(t // tn,),
        in_specs=[
            pl.BlockSpec((tn, qv.shape[1]), lambda i: (i, 0)),
            pl.BlockSpec(sub_keys.shape, lambda i: (0, 0, 0, 0)),
        ],
        out_specs=[pl.BlockSpec((tn, PEER_SLOTS), lambda i: (i, 0))] * 2,
        out_shape=[jax.ShapeDtypeStruct((t, PEER_SLOTS), jnp.int32), jax.ShapeDtypeStruct((t, PEER_SLOTS), F32)],
        compiler_params=_cparams("parallel"),
        name="peer_route",
    )(qv, sub_keys)


PEER_RING = 8
PEER_STRIDE = PEER_ROWS + 1
PEER_COLS = D_MODEL // LANES


def _peer_table(u, v):
    e, d = u.shape
    te = 256
    spec = pl.BlockSpec((te, d), lambda i: (i, 0))
    words = pl.pallas_call(
        _peer_pack_kernel,
        grid=(e // te,),
        in_specs=[spec, spec],
        out_specs=spec,
        out_shape=jax.ShapeDtypeStruct((e, d), jnp.uint32),
        compiler_params=_cparams("parallel"),
        name="peer_pack",
    )(u, v)
    return words.reshape(e * PEER_ROWS, LANES)


def _peer_pack_kernel(u_ref, v_ref, o_ref):
    o_ref[...] = pltpu.pack_elementwise([u_ref[...], v_ref[...]], packed_dtype=BF16)


def _peer_kernel(idx_ref, x_ref, g_ref, gate_ref, tab_ref, o_ref, *scratch, tb):
    bufs, sem = scratch[:PEER_RING], scratch[PEER_RING]
    ahead = PEER_RING - 1
    gnorm = g_ref[...]

    def issue(tok, j):
        for k in range(PEER_SLOTS):
            row0 = pl.multiple_of(idx_ref[tok, k], PEER_ROWS)
            pltpu.make_async_copy(tab_ref.at[pl.ds(row0, PEER_ROWS), :],
                                  bufs[j].at[pl.ds(k * PEER_STRIDE, PEER_ROWS), :], sem.at[j]).start()

    def wait(j):
        n = PEER_SLOTS * PEER_ROWS
        pltpu.make_async_copy(tab_ref.at[pl.ds(0, n), :], bufs[j].at[pl.ds(0, n), :], sem.at[j]).wait()

    def words(j, c):
        return bufs[j][pl.ds(c, PEER_SLOTS, stride=PEER_STRIDE), :]

    def compute(tok, j):
        xr = x_ref[pl.ds(tok, 1), :]
        hr = xr * lax.rsqrt(jnp.mean(xr * xr, axis=-1, keepdims=True) + NORM_EPS) * gnorm
        grow = gate_ref[pl.ds(tok, 1), :]
        gcol = jnp.broadcast_to(grow, (SUBLANES, PEER_SLOTS)).T[:, 0:1]
        acc = None
        for c in range(PEER_COLS):
            u = pltpu.unpack_elementwise(words(j, c), index=0, packed_dtype=BF16, unpacked_dtype=F32)
            term = u * hr[:, c * LANES:(c + 1) * LANES]
            acc = term if acc is None else acc + term
        pre = jnp.sum(acc, axis=1, keepdims=True)
        act = 0.5 * pre * (1.0 + lax.erf(pre * (2.0 ** -0.5))) * gcol
        outs = []
        for c in range(PEER_COLS):
            v = pltpu.unpack_elementwise(words(j, c), index=1, packed_dtype=BF16, unpacked_dtype=F32)
            outs.append(jnp.sum(v * act, axis=0, keepdims=True))
        o_ref[pl.ds(tok, 1), :] = xr + jnp.concatenate(outs, axis=1)

    for t0 in range(ahead):
        issue(t0, t0)

    def group(i, carry):
        for j in range(PEER_RING):
            tok = i * PEER_RING + j
            wait(j)
            issue(jnp.minimum(tok + ahead, tb - 1), (j + ahead) % PEER_RING)
            compute(tok, j)
        return carry

    lax.fori_loop(0, tb // PEER_RING, group, 0)
    for j in range(ahead):
        wait(j)


def _peer_experts(x, g, idx, gate, table, *, tb):
    t, d = x.shape
    assert tb % PEER_RING == 0 and t % tb == 0
    return pl.pallas_call(
        functools.partial(_peer_kernel, tb=tb),
        grid=(t // tb,),
        in_specs=[
            pl.BlockSpec((tb, PEER_SLOTS), lambda i: (i, 0), memory_space=pltpu.SMEM),
            pl.BlockSpec((tb, d), lambda i: (i, 0)),
            pl.BlockSpec((1, d), lambda i: (0, 0)),
            pl.BlockSpec((tb, PEER_SLOTS), lambda i: (i, 0)),
            pl.BlockSpec(memory_space=pl.ANY),
        ],
        out_specs=pl.BlockSpec((tb, d), lambda i: (i, 0)),
        out_shape=jax.ShapeDtypeStruct((t, d), F32),
        scratch_shapes=[pltpu.VMEM((PEER_SLOTS * PEER_STRIDE, LANES), jnp.uint32) for _ in range(PEER_RING)]
        + [pltpu.SemaphoreType.DMA((PEER_RING,))],
        compiler_params=_cparams("arbitrary"),
        name="peer_experts",
    )(idx, x, g.reshape(1, d), gate, table)


def _final_norm_kernel(x_ref, g_ref, o_ref):
    x = x_ref[...]
    o_ref[...] = x * lax.rsqrt(jnp.mean(x * x, axis=-1, keepdims=True) + NORM_EPS) * g_ref[...]


def _final_norm(x, g, *, tm):
    t, d = x.shape
    return pl.pallas_call(
        _final_norm_kernel,
        grid=(t // tm,),
        in_specs=[pl.BlockSpec((tm, d), lambda i: (i, 0)), pl.BlockSpec((1, d), lambda i: (0, 0))],
        out_specs=pl.BlockSpec((tm, d), lambda i: (i, 0)),
        out_shape=jax.ShapeDtypeStruct((t, d), F32),
        compiler_params=_cparams("parallel"),
        name="final_norm",
    )(x, g.reshape(1, d))


def _mixer_layer(x, batch, seq, norm_g, w_in, w_out, sinks, lower_bound, hgrn_gain, conv_w, i_bias, f_bias,
                 cos_t, sin_t, *, tm, sblk):
    w_in_p = jnp.concatenate([w_in[:, a:b].astype(BF16) for a, b in _IN_GROUPS]
                             + [jnp.zeros((w_in.shape[0], IN_PAD - IN_DIM), BF16)], axis=1)
    proj = _norm_matmul(x, norm_g, w_in_p, tm=2 * tm, tn=IN_PAD // 4)
    gates_t = proj[:, OFF_GATE:OFF_GATE + 2 * REC_HEADS].reshape(-1, CHUNK, 2 * REC_HEADS).transpose(0, 2, 1)
    ya = _attention(proj, sinks, cos_t, sin_t, batch, seq)
    yh = _hgrn(proj, lower_bound, hgrn_gain, batch, seq, sblk)
    ym = _mlstm(proj, gates_t, conv_w, i_bias, f_bias, batch, seq, sblk)
    return _out_proj(x, ya, yh, ym, w_out.astype(BF16), tm=tm, tn=w_out.shape[1])


def _peer_layer(x, norm_g, w_query, sub_keys, expert_u, expert_v, *, tm, tb):
    qv = _norm_matmul(x, norm_g, w_query.astype(BF16), tm=2 * tm, tn=w_query.shape[1])
    idx, gate = _route(qv, sub_keys, tn=LANES)
    return _peer_experts(x, norm_g, idx, gate, _peer_table(expert_u, expert_v), tb=tb)


def _forward(x, norm_mix, w_in, w_out, attn_sinks, hgrn_lb_logits, hgrn_out_norm, mlstm_conv, mlstm_i_bias,
             mlstm_f_bias, norm_ffn, peer_query, peer_sub_keys, peer_u, peer_v, norm_final, *, tm, sblk, tb):
    batch, seq, d = x.shape
    depth = w_in.shape[0]
    p = jax.nn.softmax(hgrn_lb_logits.astype(F32), axis=0)
    lower_bounds = jnp.cumsum(p, axis=0) - p[0:1]
    cos_t, sin_t = _rope_tables(seq)
    xt = x.reshape(batch * seq, d)
    for l in range(depth):
        xt = _mixer_layer(xt, batch, seq, norm_mix[l], w_in[l], w_out[l], attn_sinks[l], lower_bounds[l],
                          hgrn_out_norm[l], mlstm_conv[l], mlstm_i_bias[l], mlstm_f_bias[l], cos_t, sin_t,
                          tm=tm, sblk=sblk)
        xt = _peer_layer(xt, norm_ffn[l], peer_query[l], peer_sub_keys[l], peer_u[l], peer_v[l], tm=tm, tb=tb)
    return _final_norm(xt, norm_final, tm=tm).reshape(batch, seq, d)


def kernel(x, norm_mix, w_in, w_out, attn_sinks, hgrn_lb_logits, hgrn_out_norm, mlstm_conv, mlstm_i_bias, mlstm_f_bias, norm_ffn, peer_query, peer_sub_keys, peer_u, peer_v, norm_final):
    return _forward(x, norm_mix, w_in, w_out, attn_sinks, hgrn_lb_logits, hgrn_out_norm, mlstm_conv,
                    mlstm_i_bias, mlstm_f_bias, norm_ffn, peer_query, peer_sub_keys, peer_u, peer_v, norm_final,
                    tm=512, sblk=512, tb=512)
```
